```python
import math
import jax, jax.numpy as jnp
from jax import lax
import numpy as np

D_MODEL = 2048
BATCH = 32
SEQ = 256
DEPTH = 4
DEC_BATCH = 2
DEC_SEQ = 4096
PAST_LEN = 256

GRID_W = 64
N_EVEN = (DEPTH + 1) // 2
N_ODD = DEPTH // 2
MIX_HALF = D_MODEL // 2
A_HEADS = 8
A_QK_DIM = 64
A_V_DIM = MIX_HALF // A_HEADS
B_HEADS = 4
B_QK_DIM = 128
B_V_DIM = MIX_HALF // B_HEADS
RET_CHUNK = 128
C_WIDTH = MIX_HALF
C_GROUPS = 4
C_CHUNK = 128
D_WIDTH = MIX_HALF
POOL_WINDOWS = (2, 4, 8, 16)
N_POOL = 4
D_GROUP = D_WIDTH // N_POOL
PEER_HEADS = 8
PEER_NKEYS = 128
PEER_NEXP = PEER_NKEYS * PEER_NKEYS
PEER_QDIM = 256
PEER_TOPK = 16
PEER_BLOCK = 128
Q_BLOCK = 128
ROPE_BASE = 10000.0
LN_EPS = 1e-5
DEEPNORM_ALPHA = (2 * DEPTH) ** 0.25
DEEPNORM_BETA = (8 * DEPTH) ** -0.25
EVEN_SPLITS = (2 * A_HEADS * A_QK_DIM, 2 * A_HEADS * A_QK_DIM, A_HEADS * A_V_DIM,
               B_HEADS * B_QK_DIM, B_HEADS * B_QK_DIM, B_HEADS * B_V_DIM, B_HEADS * B_V_DIM)
EVEN_IN = 6144
ODD_IN = 2 * C_WIDTH + D_WIDTH

kernel_name = 'hybrid_diffusion_diffret_sgu_pool_peer_step'


def split_cols(x, sizes):
    outs, start = [], 0
    for s in sizes:
        outs.append(x[..., start:start + s])
        start += s
    return outs


def layer_norm(x, g, b):
    xf = x.astype(jnp.float32)
    mu = jnp.mean(xf, axis=-1, keepdims=True)
    var = jnp.mean(jnp.square(xf - mu), axis=-1, keepdims=True)
    return ((xf - mu) * lax.rsqrt(var + LN_EPS) * g + b).astype(x.dtype)


def head_rms_norm(x, g):
    xf = x.astype(jnp.float32)
    return xf * lax.rsqrt(jnp.mean(jnp.square(xf), axis=-1, keepdims=True) + LN_EPS) * g


def head_layer_norm(x, g):
    xf = x.astype(jnp.float32)
    mu = jnp.mean(xf, axis=-1, keepdims=True)
    var = jnp.mean(jnp.square(xf - mu), axis=-1, keepdims=True)
    return (xf - mu) * lax.rsqrt(var + LN_EPS) * g


def rope_tables(n_tokens, dim):
    rows = n_tokens // GRID_W
    row = jnp.repeat(jnp.arange(rows, dtype=jnp.float32), GRID_W)
    col = jnp.tile(jnp.arange(GRID_W, dtype=jnp.float32), rows)
    quarter = dim // 4
    freqs = ROPE_BASE ** (-jnp.arange(quarter, dtype=jnp.float32) / quarter)
    ang = jnp.concatenate([row[:, None] * freqs, col[:, None] * freqs], axis=-1)
    return jnp.cos(ang), jnp.sin(ang)


def apply_rope(x, cos, sin):
    half = x.shape[-1] // 2
    shape = (1, cos.shape[0]) + (1,) * (x.ndim - 3) + (half,)
    c, s = cos.reshape(shape), sin.reshape(shape)
    xf = x.astype(jnp.float32)
    x1, x2 = xf[..., :half], xf[..., half:]
    return jnp.concatenate([x1 * c - x2 * s, x1 * s + x2 * c], axis=-1).astype(x.dtype)


def diff_attention(q, k, v, lam):
    b, lq = q.shape[0], q.shape[1]
    nb = lq // Q_BLOCK
    qb = q.reshape((b, nb, Q_BLOCK) + q.shape[2:]).swapaxes(0, 1)
    scale = A_QK_DIM ** -0.5

    def block(qblk):
        s = jnp.einsum('bqhcd,bkhcd->bhcqk', qblk, k, preferred_element_type=jnp.float32) * scale
        p = jax.nn.softmax(s, axis=-1)
        w = p[:, :, 0] - lam * p[:, :, 1]
        return jnp.einsum('bhqk,bkhd->bqhd', w.astype(v.dtype), v)

    o = lax.map(block, qb)
    return o.swapaxes(0, 1).reshape((b, lq) + o.shape[3:])


def retention_chunkwise(q, k, v, log_gamma, s0):
    b, L, H, _ = q.shape
    dv = v.shape[-1]
    nc = L // RET_CHUNK

    def chunks(t):
        return t.reshape((b, nc, RET_CHUNK) + t.shape[2:]).swapaxes(0, 1).astype(jnp.float32)

    pos = jnp.arange(RET_CHUNK, dtype=jnp.float32)
    diff = pos[:, None] - pos[None, :]
    intra = jnp.where(diff >= 0, jnp.exp(log_gamma[:, None, None] * jnp.maximum(diff, 0.0)), 0.0)
    q_dec = jnp.exp(log_gamma[None, :] * (pos[:, None] + 1.0))
    k_dec = jnp.exp(log_gamma[None, :] * (RET_CHUNK - 1.0 - pos[:, None]))
    chunk_dec = jnp.exp(log_gamma * RET_CHUNK)

    def step(s, inp):
        qc, kc, vc = inp
        sc = jnp.einsum('bihd,bjhd->bhij', qc, kc) * intra
        inner = jnp.einsum('bhij,bjhe->bihe', sc, vc)
        cross = jnp.einsum('bihd,bhde->bihe', qc, s) * q_dec[None, :, :, None]
        s_new = s * chunk_dec[None, :, None, None] + jnp.einsum('bjhd,bjhe->bhde', kc * k_dec[None, :, :, None], vc)
        return s_new, inner + cross

    s_fin, o = lax.scan(step, s0.astype(jnp.float32), (chunks(q), chunks(k), chunks(v)))
    return o.swapaxes(0, 1).reshape(b, L, H, dv), s_fin


def bi_retention(q, k, v, lg_f, lg_b, s0_f, s0_b):
    o_f, s_f = retention_chunkwise(q, k, v, lg_f, s0_f)
    flip = lambda t: jnp.flip(t, axis=1)
    o_b, s_b = retention_chunkwise(flip(q), flip(k), flip(v), lg_b, s0_b)
    return o_f + flip(o_b), s_f, s_b


def even_mixer(h, w_in, lam_vec, a_norm_g, decay_logit, r_norm_g, lam_init,
               rope_a, rope_b, ctx_k, ctx_v, s0_f, s0_b):
    b, L, _ = h.shape
    qa, ka, va, qb, kb, vb, gb = split_cols(h @ w_in, EVEN_SPLITS)
    qa = qa.reshape(b, L, A_HEADS, 2, A_QK_DIM)
    ka = ka.reshape(b, L, A_HEADS, 2, A_QK_DIM)
    va = va.reshape(b, L, A_HEADS, A_V_DIM)
    qb = qb.reshape(b, L, B_HEADS, B_QK_DIM)
    kb = kb.reshape(b, L, B_HEADS, B_QK_DIM) * (B_QK_DIM ** -0.5)
    vb = vb.reshape(b, L, B_HEADS, B_V_DIM)
    ka_pos = ka
    if rope_a is not None:
        qa, ka_pos = apply_rope(qa, *rope_a), apply_rope(ka, *rope_a)
        qb, kb = apply_rope(qb, *rope_b), apply_rope(kb, *rope_b)
    if ctx_k is None:
        keys, vals = ka_pos, va
    else:
        keys = jnp.concatenate([ctx_k.astype(ka_pos.dtype), ka_pos], axis=1)
        vals = jnp.concatenate([ctx_v.astype(va.dtype), va], axis=1)
    lv = lam_vec.astype(jnp.float32)
    lam = jnp.exp(jnp.sum(lv[0] * lv[1])) - jnp.exp(jnp.sum(lv[2] * lv[3])) + lam_init
    oa = diff_attention(qa, keys, vals, lam)
    oa = head_rms_norm(oa, a_norm_g) * (1.0 - lam_init)
    log_g = jax.nn.log_sigmoid(decay_logit.astype(jnp.float32))
    ob, s_f, s_b = bi_retention(qb, kb, vb, log_g[0], log_g[1], s0_f, s0_b)
    ob = head_layer_norm(ob, r_norm_g) * jax.nn.silu(gb.astype(jnp.float32)).reshape(b, L, B_HEADS, B_V_DIM)
    mix = jnp.concatenate([oa.reshape(b, L, -1), ob.reshape(b, L, -1)], axis=-1).astype(h.dtype)
    return mix, ka, va, s_f, s_b


def spatial_gating(u, v, g, bb, w_s, b_s):
    b, L, _ = u.shape
    vn = layer_norm(v, g, bb)
    vg = vn.reshape(b, L // C_CHUNK, C_CHUNK, C_GROUPS, C_WIDTH // C_GROUPS)
    mixed = jnp.einsum('gij,bnjgc->bnigc', w_s, vg) + b_s.T[None, None, :, :, None]
    return u * mixed.reshape(b, L, C_WIDTH)


def multiscale_pool(p, p_w, p_scale):
    b, L, _ = p.shape
    pg = p.reshape(b, L, N_POOL, D_GROUP).astype(jnp.float32)
    cs = jnp.concatenate([jnp.zeros((b, 1, N_POOL, D_GROUP), jnp.float32), jnp.cumsum(pg, axis=1)], axis=1)
    t = jnp.arange(L)
    outs = []
    for gi, w in enumerate(POOL_WINDOWS):
        lo = jnp.clip(t - w // 2, 0, L - 1)
        hi = jnp.clip(t + w // 2 - 1, 0, L - 1)
        cs_g = cs[:, :, gi]
        mean = (cs_g[:, hi + 1] - cs_g[:, lo]) / (hi - lo + 1).astype(jnp.float32)[None, :, None]
        outs.append(mean - pg[:, :, gi])
    pooled = jnp.stack(outs, axis=2)
    mixed = jnp.einsum('blgc,gcd->blgd', pooled, p_w.astype(jnp.float32))
    return mixed.reshape(b, L, D_WIDTH) * p_scale


def odd_mixer(h, w_in, sg_g, sg_b, w_s, b_s, p_w, p_scale):
    u, v, p = split_cols(h @ w_in, (C_WIDTH, C_WIDTH, D_WIDTH))
    oc = spatial_gating(u, v, sg_g, sg_b, w_s, b_s)
    od = multiscale_pool(p, p_w, p_scale)
    return jnp.concatenate([oc.astype(h.dtype), od.astype(h.dtype)], axis=-1)


def peer(h, w_q, sub_keys, u_tab, v_tab):
    b, L, d = h.shape
    x = h.reshape(-1, d)
    T = x.shape[0]
    q = (x @ w_q).reshape(T, PEER_HEADS, 2, PEER_QDIM // 2)
    scores = jnp.einsum('thpk,hpnk->thpn', q, sub_keys, preferred_element_type=jnp.float32)
    s_top, i_top = lax.top_k(scores, PEER_TOPK)
    cand = s_top[:, :, 0, :, None] + s_top[:, :, 1, None, :]
    cand_idx = i_top[:, :, 0, :, None] * PEER_NKEYS + i_top[:, :, 1, None, :]
    best, pos = lax.top_k(cand.reshape(T, PEER_HEADS, PEER_TOPK * PEER_TOPK), PEER_TOPK)
    idx = jnp.take_along_axis(cand_idx.reshape(T, PEER_HEADS, PEER_TOPK * PEER_TOPK), pos, axis=-1)
    gates = jax.nn.softmax(best, axis=-1)
    nb = T // PEER_BLOCK
    n_sel = PEER_HEADS * PEER_TOPK
    xb = x.reshape(nb, PEER_BLOCK, d)
    ib = idx.reshape(nb, PEER_BLOCK, n_sel)
    gb = gates.reshape(nb, PEER_BLOCK, n_sel)

    def block(args):
        xk, ik, gk = args
        a = jax.nn.gelu(jnp.einsum('td,ted->te', xk, u_tab[ik], preferred_element_type=jnp.float32))
        return jnp.einsum('te,ted->td', (gk * a).astype(v_tab.dtype), v_tab[ik])

    y = lax.map(block, (xb, ib, gb))
    return y.reshape(b, L, d).astype(h.dtype)


def setup_inputs(seed: int = 0) -> dict:
    key = jax.random.key(seed)
    ks = iter(jax.random.split(key, 32))
    f32 = jnp.float32

    def nrm(shape, scale):
        return jax.random.normal(next(ks), shape, f32) * scale

    inv_d = D_MODEL ** -0.5
    decay_base = jnp.log(2.0 ** (5.0 + jnp.arange(B_HEADS, dtype=f32)) - 1.0)
    return {
        'x_prompt': nrm((BATCH, SEQ, D_MODEL), 1.0),
        'x_sample': nrm((DEC_BATCH, DEC_SEQ, D_MODEL), 1.0),
        'cache_attn_k': nrm((DEC_BATCH, N_EVEN, A_HEADS, PAST_LEN, 2 * A_QK_DIM), 1.0),
        'cache_attn_v': nrm((DEC_BATCH, N_EVEN, A_HEADS, PAST_LEN, A_V_DIM), 1.0),
        'state_retention': nrm((DEC_BATCH, N_EVEN, 2, B_HEADS, B_QK_DIM, B_V_DIM), 0.5),
        'c': nrm((DEC_BATCH, D_MODEL), 1.0),
        'c_ctx': nrm((D_MODEL,), 1.0),
        'w_ada': nrm((DEPTH, D_MODEL, 6 * D_MODEL), inv_d),
        'b_ada': nrm((DEPTH, 6 * D_MODEL), 0.02),
        'ln_g': 1.0 + nrm((DEPTH, 2, D_MODEL), 0.02),
        'ln_b': nrm((DEPTH, 2, D_MODEL), 0.02),
        'w_in_even': nrm((N_EVEN, D_MODEL, EVEN_IN), inv_d),
        'diff_lambda': nrm((N_EVEN, 4, A_QK_DIM), 0.1),
        'diff_norm_g': 1.0 + nrm((N_EVEN, A_HEADS, A_V_DIM), 0.02),
        'ret_decay_logit': decay_base + nrm((N_EVEN, 2, B_HEADS), 0.1),
        'ret_norm_g': 1.0 + nrm((N_EVEN, B_HEADS, B_V_DIM), 0.02),
        'w_in_odd': nrm((N_ODD, D_MODEL, ODD_IN), inv_d),
        'sgu_ln_g': 1.0 + nrm((N_ODD, C_WIDTH), 0.02),
        'sgu_ln_b': nrm((N_ODD, C_WIDTH), 0.02),
        'sgu_w': nrm((N_ODD, C_GROUPS, C_CHUNK, C_CHUNK), C_CHUNK ** -0.5),
        'sgu_b': 1.0 + nrm((N_ODD, C_GROUPS, C_CHUNK), 0.02),
        'pool_w': nrm((N_ODD, N_POOL, D_GROUP, D_GROUP), D_GROUP ** -0.5),
        'pool_scale': 1.0 + nrm((N_ODD, D_WIDTH), 0.02),
        'w_out': nrm((DEPTH, D_MODEL, D_MODEL), DEEPNORM_BETA * inv_d),
        'peer_wq': nrm((DEPTH, D_MODEL, PEER_HEADS * PEER_QDIM), inv_d),
        'peer_keys': nrm((DEPTH, PEER_HEADS, 2, PEER_NKEYS, PEER_QDIM // 2), (PEER_QDIM // 2) ** -0.5),
        'peer_u': nrm((DEPTH, PEER_NEXP, D_MODEL), inv_d),
        'peer_v': nrm((DEPTH, PEER_NEXP, D_MODEL), DEEPNORM_BETA),
    }


def reference(x_prompt, x_sample, cache_attn_k, cache_attn_v, state_retention, c, c_ctx,
              w_ada, b_ada, ln_g, ln_b, w_in_even, diff_lambda, diff_norm_g, ret_decay_logit,
              ret_norm_g, w_in_odd, sgu_ln_g, sgu_ln_b, sgu_w, sgu_b, pool_w, pool_scale,
              w_out, peer_wq, peer_keys, peer_u, peer_v):

    def trunk(x, cond, latent):
        b, L, _ = x.shape
        rope_a = rope_tables(L, A_QK_DIM) if latent else None
        rope_b = rope_tables(L, B_QK_DIM) if latent else None
        ks_out, vs_out, ss_out = [], [], []
        for l in range(DEPTH):
            mod = (jax.nn.silu(cond) @ w_ada[l] + b_ada[l]).reshape(cond.shape[:2] + (6, D_MODEL))
            sh1, sc1, g1, sh2, sc2, g2 = (mod[:, :, j] for j in range(6))
            h = x * (1.0 + sc1) + sh1
            if l % 2 == 0:
                e = l // 2
                if latent:
                    lc = cache_attn_k.shape[3]
                    ctx_k = cache_attn_k[:, e].transpose(0, 2, 1, 3).reshape(b, lc, A_HEADS, 2, A_QK_DIM)
                    ctx_v = cache_attn_v[:, e].transpose(0, 2, 1, 3)
                    s0_f, s0_b = state_retention[:, e, 0], state_retention[:, e, 1]
                else:
                    ctx_k = ctx_v = None
                    s0_f = s0_b = jnp.zeros((b, B_HEADS, B_QK_DIM, B_V_DIM), jnp.float32)
                lam_init = 0.8 - 0.6 * math.exp(-0.3 * l)
                mix, ka, va, s_f, s_b = even_mixer(
                    h, w_in_even[e], diff_lambda[e], diff_norm_g[e], ret_decay_logit[e], ret_norm_g[e],
                    lam_init, rope_a, rope_b, ctx_k, ctx_v, s0_f, s0_b)
                if not latent:
                    ks_out.append(ka.reshape(b, L, A_HEADS, 2 * A_QK_DIM).transpose(0, 2, 1, 3))
                    vs_out.append(va.transpose(0, 2, 1, 3))
                    ss_out.append(jnp.stack([s_f, s_b], axis=1))
            else:
                o = l // 2
                mix = odd_mixer(h, w_in_odd[o], sgu_ln_g[o], sgu_ln_b[o], sgu_w[o], sgu_b[o],
                                pool_w[o], pool_scale[o])
            x = layer_norm(DEEPNORM_ALPHA * x + g1 * (mix @ w_out[l]), ln_g[l, 0], ln_b[l, 0])
            h2 = x * (1.0 + sc2) + sh2
            y = peer(h2, peer_wq[l], peer_keys[l], peer_u[l], peer_v[l])
            x = layer_norm(DEEPNORM_ALPHA * x + g2 * y, ln_g[l, 1], ln_b[l, 1])
        return x, ks_out, vs_out, ss_out

    y_prompt, ks, vs, ss = trunk(x_prompt, c_ctx[None, None, :], False)
    new_attn_k = jnp.stack(ks, axis=1)
    new_attn_v = jnp.stack(vs, axis=1)
    new_ret_state = jnp.stack(ss, axis=1)
    y_sample, _, _, _ = trunk(x_sample, c[:, None, :], True)
    return (y_prompt, y_sample, new_attn_k, new_attn_v, new_ret_state)
```

```python
import functools
import math

import jax
import jax.numpy as jnp
from jax import lax
from jax.experimental import pallas as pl
from jax.experimental.pallas import tpu as pltpu

F32 = jnp.float32
BF16 = jnp.bfloat16

D_MODEL = 2048
BATCH = 32
SEQ = 256
DEPTH = 4
DEC_BATCH = 2
DEC_SEQ = 4096
PAST_LEN = 256
GRID_W = 64
N_EVEN = (DEPTH + 1) // 2
MIX_HALF = D_MODEL // 2
A_HEADS = 8
A_QK_DIM = 64
A_V_DIM = MIX_HALF // A_HEADS
B_HEADS = 4
B_QK_DIM = 128
B_V_DIM = MIX_HALF // B_HEADS
RET_CHUNK = 128
C_WIDTH = MIX_HALF
C_GROUPS = 4
C_CHUNK = 128
D_WIDTH = MIX_HALF
POOL_WINDOWS = (2, 4, 8, 16)
N_POOL = 4
D_GROUP = D_WIDTH // N_POOL
PEER_HEADS = 8
PEER_NKEYS = 128
PEER_NEXP = PEER_NKEYS * PEER_NKEYS
PEER_QDIM = 256
PEER_TOPK = 16
ROPE_BASE = 10000.0
LN_EPS = 1e-5
DEEPNORM_ALPHA = (2 * DEPTH) ** 0.25
EVEN_IN = 6144
ODD_IN = 2 * C_WIDTH + D_WIDTH

T_CTX = BATCH * SEQ
T_LAT = DEC_BATCH * DEC_SEQ
T_ALL = T_CTX + T_LAT
N_COND = 1 + DEC_BATCH
COND_PAD = 8

LANES = 128
VMEM_LIMIT = 56 * 1024 * 1024


def _cparams(sem):
    return pltpu.CompilerParams(dimension_semantics=sem, vmem_limit_bytes=VMEM_LIMIT)


def _mod_row(i, tm):
    start = i * tm
    return jnp.where(start < T_CTX, 0, 1 + (start - T_CTX) // DEC_SEQ)


def _ada_kernel(cond_ref, w_ref, b_ref, o_ref):
    c = cond_ref[...]
    s = (c * jax.nn.sigmoid(c)).astype(BF16)
    o_ref[...] = jnp.dot(s, w_ref[...].astype(BF16), preferred_element_type=F32) + b_ref[...]


def _ada(cond, w_ada, b_ada, tn=1024):
    depth, d, n = w_ada.shape
    return pl.pallas_call(
        _ada_kernel,
        grid=(depth, n // tn),
        in_specs=[pl.BlockSpec((COND_PAD, d), lambda l, j: (0, 0)),
                  pl.BlockSpec((None, d, tn), lambda l, j: (l, 0, j)),
                  pl.BlockSpec((None, 1, tn), lambda l, j: (l, 0, j))],
        out_specs=pl.BlockSpec((None, COND_PAD, tn), lambda l, j: (l, 0, j)),
        out_shape=jax.ShapeDtypeStruct((depth, COND_PAD, n), F32),
        compiler_params=_cparams(("parallel", "parallel")),
        name="ada",
    )(cond, w_ada, b_ada.reshape(depth, 1, n))


def _inproj_kernel(x_ref, mod_ref, w_ref, o_ref, h_ref):
    @pl.when(pl.program_id(1) == 0)
    def _():
        sh = mod_ref[0:1, :]
        sc = mod_ref[1:2, :]
        h_ref[...] = (x_ref[...] * (1.0 + sc) + sh).astype(BF16)

    o_ref[...] = jnp.dot(h_ref[...], w_ref[...], preferred_element_type=F32)


def _inproj(x, mods_l, w, tm=1024, tn=1024):
    t, d = x.shape
    n = w.shape[1]
    return pl.pallas_call(
        _inproj_kernel,
        grid=(t // tm, n // tn),
        in_specs=[pl.BlockSpec((tm, d), lambda i, j: (i, 0)),
                  pl.BlockSpec((None, 6, d), lambda i, j: (_mod_row(i, tm), 0, 0)),
                  pl.BlockSpec((d, tn), lambda i, j: (0, j))],
        out_specs=pl.BlockSpec((tm, tn), lambda i, j: (i, j)),
        out_shape=jax.ShapeDtypeStruct((t, n), F32),
        scratch_shapes=[pltpu.VMEM((tm, d), BF16)],
        compiler_params=_cparams(("parallel", "arbitrary")),
        name="inproj",
    )(x, mods_l, w)


def _layer_norm_rows(z, g, b):
    mu = jnp.mean(z, axis=-1, keepdims=True)
    zc = z - mu
    var = jnp.mean(zc * zc, axis=-1, keepdims=True)
    return zc * lax.rsqrt(var + LN_EPS) * g + b


def _outproj_kernel(ma_ref, mb_ref, x_ref, mod_ref, wa_ref, wb_ref, g_ref, b_ref, x1_ref, h2t_ref):
    o = jnp.dot(ma_ref[...], wa_ref[...], preferred_element_type=F32)
    o = o + jnp.dot(mb_ref[...], wb_ref[...], preferred_element_type=F32)
    g1 = mod_ref[2:3, :]
    x1 = _layer_norm_rows(DEEPNORM_ALPHA * x_ref[...] + g1 * o, g_ref[...], b_ref[...])
    x1_ref[...] = x1
    h2 = x1 * (1.0 + mod_ref[4:5, :]) + mod_ref[3:4, :]
    h2t_ref[...] = h2.T.astype(BF16)


def _outproj(mix_a, mix_b, x, mods_l, w_out, ln_g, ln_b, tm=512):
    t, d = x.shape
    ka = mix_a.shape[1]
    kb = mix_b.shape[1]
    return pl.pallas_call(
        _outproj_kernel,
        grid=(t // tm,),
        in_specs=[pl.BlockSpec((tm, ka), lambda i: (i, 0)),
                  pl.BlockSpec((tm, kb), lambda i: (i, 0)),
                  pl.BlockSpec((tm, d), lambda i: (i, 0)),
                  pl.BlockSpec((None, 6, d), lambda i: (_mod_row(i, tm), 0, 0)),
                  pl.BlockSpec((ka, d), lambda i: (0, 0)),
                  pl.BlockSpec((kb, d), lambda i: (1, 0)),
                  pl.BlockSpec((1, d), lambda i: (0, 0)),
                  pl.BlockSpec((1, d), lambda i: (0, 0))],
        out_specs=[pl.BlockSpec((tm, d), lambda i: (i, 0)),
                   pl.BlockSpec((d, tm), lambda i: (0, i))],
        out_shape=[jax.ShapeDtypeStruct((t, d), F32),
                   jax.ShapeDtypeStruct((d, t), BF16)],
        compiler_params=_cparams(("parallel",)),
        name="outproj_ln",
    )(mix_a, mix_b, x, mods_l, w_out, w_out, ln_g, ln_b)


def _split_bf16(x):
    hi = x.astype(BF16)
    lo = (x - hi.astype(F32)).astype(BF16)
    return hi, lo


def _wqk_kernel(k_ref, wq_ref, o_ref):
    kh, kl = _split_bf16(k_ref[...])
    wh, wl = _split_bf16(wq_ref[...])
    nt = (((1,), (1,)), ((), ()))
    acc = lax.dot_general(kh, wh, nt, preferred_element_type=F32)
    acc = acc + lax.dot_general(kh, wl, nt, preferred_element_type=F32)
    acc = acc + lax.dot_general(kl, wh, nt, preferred_element_type=F32)
    o_ref[...] = acc.astype(BF16)


def _wqk_fold(keys_l, wq_l):
    h, p, nk, half = keys_l.shape
    d = wq_l.shape[0]
    return pl.pallas_call(
        _wqk_kernel,
        grid=(h * p,),
        in_specs=[pl.BlockSpec((None, nk, half), lambda i: (i, 0, 0)),
                  pl.BlockSpec((d, half), lambda i: (0, i))],
        out_specs=pl.BlockSpec((nk, d), lambda i: (i, 0)),
        out_shape=jax.ShapeDtypeStruct((h * p * nk, d), BF16),
        compiler_params=_cparams(("parallel",)),
        name="peer_wqk_fold",
    )(keys_l.reshape(h * p, nk, half), wq_l)


def _scores_kernel(w_ref, h_ref, o_ref):
    o_ref[...] = jnp.dot(w_ref[...], h_ref[...], preferred_element_type=F32)


def _scores(wqk_t, h2t, tn=512):
    n, d = wqk_t.shape
    t = h2t.shape[1]
    return pl.pallas_call(
        _scores_kernel,
        grid=(t // tn,),
        in_specs=[pl.BlockSpec((n, d), lambda i: (0, 0)),
                  pl.BlockSpec((d, tn), lambda i: (0, i))],
        out_specs=pl.BlockSpec((n, tn), lambda i: (0, i)),
        out_shape=jax.ShapeDtypeStruct((n, t), F32),
        compiler_params=_cparams(("parallel",)),
        name="peer_scores",
    )(wqk_t, h2t)


def _gelu_tanh(x):
    c = math.sqrt(2.0 / math.pi)
    return 0.5 * x * (1.0 + jnp.tanh(c * (x + 0.044715 * (x * x * x))))


def _peer_kernel(h2t_ref, u_ref, vt_ref, r2_ref, e2_ref, rows_ref, x1_ref, mod_ref, g_ref, b_ref,
                 o_ref, acc_ref, w_ref, *, n_i, n_heads, nk):
    eb = pl.program_id(1)

    @pl.when(eb == 0)
    def _():
        acc_ref[...] = jnp.zeros_like(acc_ref)

    a = jnp.dot(u_ref[...], h2t_ref[...], preferred_element_type=F32)
    for ii in range(n_i):
        gate = None
        for h in range(n_heads):
            lim = rows_ref[ii, h:h + 1, :].astype(BF16)
            w1 = rows_ref[ii, n_heads + h:n_heads + h + 1, :].astype(BF16)
            term = jnp.where(r2_ref[h] < lim, e2_ref[h], jnp.zeros((), BF16)) * w1
            gate = term if gate is None else gate + term
        act = _gelu_tanh(a[ii * nk:(ii + 1) * nk, :]).astype(BF16)
        w_ref[ii * nk:(ii + 1) * nk, :] = act * gate
    acc_ref[...] += jnp.dot(vt_ref[...], w_ref[...], preferred_element_type=F32)

    @pl.when(eb == pl.num_programs(1) - 1)
    def _():
        g2 = mod_ref[5:6, :]
        tm = acc_ref.shape[1]
        for c in range(tm // LANES):
            rows = slice(c * LANES, (c + 1) * LANES)
            y = acc_ref[:, rows].T
            z = DEEPNORM_ALPHA * x1_ref[rows, :] + g2 * y
            o_ref[rows, :] = _layer_norm_rows(z, g_ref[...], b_ref[...])


def _peer_dense(h2t, u, vt, r2t, e2t, rows_t, x1, mods_l, ln_g, ln_b, tm=512, n_i=4):
    d, t = h2t.shape
    nexp = u.shape[0]
    n_heads, nk, _ = r2t.shape
    te = n_i * nk
    kern = functools.partial(_peer_kernel, n_i=n_i, n_heads=n_heads, nk=nk)
    return pl.pallas_call(
        kern,
        grid=(t // tm, nexp // te),
        in_specs=[pl.BlockSpec((d, tm), lambda i, e: (0, i)),
                  pl.BlockSpec((te, d), lambda i, e: (e, 0)),
                  pl.BlockSpec((d, te), lambda i, e: (0, e)),
                  pl.BlockSpec((n_heads, nk, tm), lambda i, e: (0, 0, i)),
                  pl.BlockSpec((n_heads, nk, tm), lambda i, e: (0, 0, i)),
                  pl.BlockSpec((n_i, 2 * n_heads, tm), lambda i, e: (e, 0, i)),
                  pl.BlockSpec((tm, d), lambda i, e: (i, 0)),
                  pl.BlockSpec((None, 6, d), lambda i, e: (_mod_row(i, tm), 0, 0)),
                  pl.BlockSpec((1, d), lambda i, e: (0, 0)),
                  pl.BlockSpec((1, d), lambda i, e: (0, 0))],
        out_specs=pl.BlockSpec((tm, d), lambda i, e: (i, 0)),
        out_shape=jax.ShapeDtypeStruct((t, d), F32),
        scratch_shapes=[pltpu.VMEM((d, tm), F32), pltpu.VMEM((te, tm), BF16)],
        compiler_params=_cparams(("parallel", "arbitrary")),
        name="peer_dense",
    )(h2t, u, vt, r2t, e2t, rows_t, x1, mods_l, ln_g, ln_b)


def _peer_routing(scores_t):
    t = scores_t.shape[1]
    k = PEER_TOPK
    s = scores_t.reshape(PEER_HEADS, 2, PEER_NKEYS, t).transpose(3, 0, 1, 2)
    s_top, i_top = lax.top_k(s, k)
    cand = s_top[:, :, 0, :, None] + s_top[:, :, 1, None, :]
    best, pos = lax.top_k(cand.reshape(t, PEER_HEADS, k * k), k)
    a_rank = pos // k
    n_a = jnp.sum((a_rank[..., None] == jnp.arange(k)).astype(F32), axis=-2)
    gates = jax.nn.softmax(best, axis=-1)
    m1 = s_top[:, :, 0, 0]
    m2 = s_top[:, :, 1, 0]
    z = jnp.sum(jnp.exp(best - (m1 + m2)[..., None]), axis=-1)
    del gates
    oh1 = (i_top[:, :, 0, :, None] == jnp.arange(PEER_NKEYS)).astype(F32)
    oh2 = (i_top[:, :, 1, :, None] == jnp.arange(PEER_NKEYS)).astype(F32)
    lim1 = jnp.einsum('thk,thkn->thn', n_a, oh1)
    rank2 = float(k) + jnp.einsum('k,thkn->thn', jnp.arange(k, dtype=F32) - float(k), oh2)
    w1 = jnp.exp(s[:, :, 0, :] - m1[..., None]) / z[..., None]
    e2 = jnp.exp(s[:, :, 1, :] - m2[..., None])
    r2t = rank2.transpose(1, 2, 0).astype(BF16)
    e2t = e2.transpose(1, 2, 0).astype(BF16)
    rows_t = jnp.concatenate([lim1, w1], axis=1).transpose(2, 1, 0)
    return r2t, e2t, rows_t


def _rope_tables(n_tokens, dim):
    rows = n_tokens // GRID_W
    row = jnp.repeat(jnp.arange(rows, dtype=F32), GRID_W)
    col = jnp.tile(jnp.arange(GRID_W, dtype=F32), rows)
    quarter = dim // 4
    freqs = ROPE_BASE ** (-jnp.arange(quarter, dtype=F32) / quarter)
    ang = jnp.concatenate([row[:, None] * freqs, col[:, None] * freqs], axis=-1)
    cos, sin = jnp.cos(ang), jnp.sin(ang)
    reps = LANES // dim
    cos_full = jnp.tile(jnp.concatenate([cos, cos], axis=-1), (1, reps))
    sin_full = jnp.tile(jnp.concatenate([-sin, sin], axis=-1), (1, reps))
    return cos_full, sin_full


def _rope_kernel(x_ref, cos_ref, sin_ref, o_ref, *, half, scale):
    x = x_ref[...]
    w = x.shape[1]
    lane = lax.broadcasted_iota(jnp.int32, x.shape, 1)
    partner = jnp.where((lane % (2 * half)) < half, pltpu.roll(x, w - half, 1), pltpu.roll(x, half, 1))
    o_ref[...] = ((x * cos_ref[...] + partner * sin_ref[...]) * scale).astype(o_ref.dtype)


def _rope(proj, col0, ncols, row0, nrows, cos_full, sin_full, half, scale=1.0, tm=512):
    kern = functools.partial(_rope_kernel, half=half, scale=scale)
    cb0 = col0 // LANES
    rb0 = row0 // tm
    return pl.pallas_call(
        kern,
        grid=(nrows // tm, ncols // LANES),
        in_specs=[pl.BlockSpec((tm, LANES), lambda i, j: (rb0 + i, cb0 + j)),
                  pl.BlockSpec((tm, LANES), lambda i, j: (i % (cos_full.shape[0] // tm), 0)),
                  pl.BlockSpec((tm, LANES), lambda i, j: (i % (cos_full.shape[0] // tm), 0))],
        out_specs=pl.BlockSpec((tm, LANES), lambda i, j: (i, j)),
        out_shape=jax.ShapeDtypeStruct((nrows, ncols), BF16),
        compiler_params=_cparams(("parallel", "parallel")),
        name="rope",
    )(proj, cos_full, sin_full)


def _attn_kernel(*refs, tq, l_self, l_ctx, lam_init):
    if l_ctx:
        lam_ref, q_ref, k_ref, v_ref, ck_ref, cv_ref, g_ref, o_ref, ks_ref, vs_ref = refs
    else:
        lam_ref, q_ref, k_ref, v_ref, g_ref, o_ref, ks_ref, vs_ref = refs

    @pl.when(pl.program_id(2) == 0)
    def _():
        if l_ctx:
            ks_ref[0:l_ctx, :] = ck_ref[...].astype(BF16)
            vs_ref[0:l_ctx, :] = cv_ref[...].astype(BF16)
        ks_ref[l_ctx:l_ctx + l_self, :] = k_ref[...].astype(BF16)
        vs_ref[l_ctx:l_ctx + l_self, :] = v_ref[...].astype(BF16)

    lv = lam_ref[...]
    lam = (jnp.exp(jnp.sum(lv[0:1, :] * lv[1:2, :], axis=-1, keepdims=True))
           - jnp.exp(jnp.sum(lv[2:3, :] * lv[3:4, :], axis=-1, keepdims=True)) + lam_init)
    q = q_ref[...].astype(F32) * (A_QK_DIM ** -0.5)
    lane = lax.broadcasted_iota(jnp.int32, q.shape, 1)
    first = lane < A_QK_DIM
    q0 = jnp.where(first, q, 0.0).astype(BF16)
    q1 = jnp.where(first, 0.0, q).astype(BF16)
    qs = jnp.concatenate([q0, q1], axis=0)
    s = lax.dot_general(qs, ks_ref[...], (((1,), (1,)), ((), ())), preferred_element_type=F32)
    m = jnp.max(s, axis=-1, keepdims=True)
    p = jnp.exp(s - m)
    p = p / jnp.sum(p, axis=-1, keepdims=True)
    w = (p[0:tq, :] - lam * p[tq:2 * tq, :]).astype(BF16)
    o = jnp.dot(w, vs_ref[...], preferred_element_type=F32)
    o = o * lax.rsqrt(jnp.mean(o * o, axis=-1, keepdims=True) + LN_EPS) * g_ref[...] * (1.0 - lam_init)
    o_ref[...] = o.astype(o_ref.dtype)


def _diff_attention(q_src, k_src, v_src, nb, l_self, lam_vec, norm_g, lam_init, ctx_k=None, ctx_v=None, tq=256):
    (q_arr, q_col0, q_row0), (k_arr, k_col0, k_row0), (v_arr, v_col0, v_row0) = q_src, k_src, v_src
    l_ctx = 0 if ctx_k is None else ctx_k.shape[2]
    nq = l_self // tq
    qb0, kb0, vb0 = q_col0 // LANES, k_col0 // LANES, v_col0 // LANES
    rq0, rk0, rv0 = q_row0 // tq, k_row0 // l_self, v_row0 // l_self
    kern = functools.partial(_attn_kernel, tq=tq, l_self=l_self, l_ctx=l_ctx, lam_init=lam_init)
    in_specs = [pl.BlockSpec((4, A_QK_DIM), lambda b, h, i: (0, 0)),
                pl.BlockSpec((tq, LANES), lambda b, h, i: (rq0 + b * nq + i, qb0 + h)),
                pl.BlockSpec((l_self, LANES), lambda b, h, i: (rk0 + b, kb0 + h)),
                pl.BlockSpec((l_self, LANES), lambda b, h, i: (rv0 + b, vb0 + h))]
    args = [lam_vec, q_arr, k_arr, v_arr]
    if l_ctx:
        in_specs += [pl.BlockSpec((None, None, l_ctx, LANES), lambda b, h, i: (b, h, 0, 0)),
                     pl.BlockSpec((None, None, l_ctx, LANES), lambda b, h, i: (b, h, 0, 0))]
        args += [ctx_k, ctx_v]
    in_specs += [pl.BlockSpec((None, 1, LANES), lambda b, h, i: (h, 0, 0))]
    args += [norm_g.reshape(A_HEADS, 1, A_V_DIM)]
    return pl.pallas_call(
        kern,
        grid=(nb, A_HEADS, nq),
        in_specs=in_specs,
        out_specs=pl.BlockSpec((tq, LANES), lambda b, h, i: (b * nq + i, h)),
        out_shape=jax.ShapeDtypeStruct((nb * l_self, A_HEADS * A_V_DIM), BF16),
        scratch_shapes=[pltpu.VMEM((l_ctx + l_self, LANES), BF16),
                        pltpu.VMEM((l_ctx + l_self, LANES), BF16)],
        compiler_params=_cparams(("parallel", "parallel", "arbitrary")),
        name="diff_attention",
    )(*args)


def _head_layer_norm(x, g):
    mu = jnp.mean(x, axis=-1, keepdims=True)
    var = jnp.mean(jnp.square(x - mu), axis=-1, keepdims=True)
    return (x - mu) * lax.rsqrt(var + LN_EPS) * g


def _retention_chunkwise(q, k, v, log_gamma, s0):
    b, L, H, _ = q.shape
    dv = v.shape[-1]
    nc = L // RET_CHUNK

    def chunks(t):
        return t.reshape((b, nc, RET_CHUNK) + t.shape[2:]).swapaxes(0, 1).astype(F32)

    pos = jnp.arange(RET_CHUNK, dtype=F32)
    diff = pos[:, None] - pos[None, :]
    intra = jnp.where(diff >= 0, jnp.exp(log_gamma[:, None, None] * jnp.maximum(diff, 0.0)), 0.0)
    q_dec = jnp.exp(log_gamma[None, :] * (pos[:, None] + 1.0))
    k_dec = jnp.exp(log_gamma[None, :] * (RET_CHUNK - 1.0 - pos[:, None]))
    chunk_dec = jnp.exp(log_gamma * RET_CHUNK)

    def step(s, inp):
        qc, kc, vc = inp
        sc = jnp.einsum('bihd,bjhd->bhij', qc, kc) * intra
        inner = jnp.einsum('bhij,bjhe->bihe', sc, vc)
        cross = jnp.einsum('bihd,bhde->bihe', qc, s) * q_dec[None, :, :, None]
        s_new = s * chunk_dec[None, :, None, None] + jnp.einsum('bjhd,bjhe->bhde', kc * k_dec[None, :, :, None], vc)
        return s_new, inner + cross

    s_fin, o = lax.scan(step, s0.astype(F32), (chunks(q), chunks(k), chunks(v)))
    return o.swapaxes(0, 1).reshape(b, L, H, dv), s_fin


def _retention_jnp(qb, kb, vb, gb, decay_logit, norm_g, s0_f, s0_b):
    b, L = qb.shape[:2]
    log_g = jax.nn.log_sigmoid(decay_logit.astype(F32))
    o_f, s_f = _retention_chunkwise(qb, kb, vb, log_g[0], s0_f)
    flip = lambda t: jnp.flip(t, axis=1)
    o_b, s_b = _retention_chunkwise(flip(qb), flip(kb), flip(vb), log_g[1], s0_b)
    ob = o_f + flip(o_b)
    ob = _head_layer_norm(ob, norm_g) * jax.nn.silu(gb).reshape(b, L, B_HEADS, B_V_DIM)
    return ob.reshape(b * L, -1).astype(BF16), s_f, s_b


def _spatial_gating_jnp(u, v, g, bb, w_s, b_s):
    b, L, _ = u.shape
    mu = jnp.mean(v, axis=-1, keepdims=True)
    var = jnp.mean(jnp.square(v - mu), axis=-1, keepdims=True)
    vn = (v - mu) * lax.rsqrt(var + LN_EPS) * g + bb
    vg = vn.reshape(b, L // C_CHUNK, C_CHUNK, C_GROUPS, C_WIDTH // C_GROUPS)
    mixed = jnp.einsum('gij,bnjgc->bnigc', w_s, vg) + b_s.T[None, None, :, :, None]
    return u * mixed.reshape(b, L, C_WIDTH)


def _multiscale_pool_jnp(p, p_w, p_scale):
    b, L, _ = p.shape
    pg = p.reshape(b, L, N_POOL, D_GROUP).astype(F32)
    cs = jnp.concatenate([jnp.zeros((b, 1, N_POOL, D_GROUP), F32), jnp.cumsum(pg, axis=1)], axis=1)
    t = jnp.arange(L)
    outs = []
    for gi, w in enumerate(POOL_WINDOWS):
        lo = jnp.clip(t - w // 2, 0, L - 1)
        hi = jnp.clip(t + w // 2 - 1, 0, L - 1)
        cs_g = cs[:, :, gi]
        mean = (cs_g[:, hi + 1] - cs_g[:, lo]) / (hi - lo + 1).astype(F32)[None, :, None]
        outs.append(mean - pg[:, :, gi])
    pooled = jnp.stack(outs, axis=2)
    mixed = jnp.einsum('blgc,gcd->blgd', pooled, p_w.astype(F32))
    return mixed.reshape(b, L, D_WIDTH) * p_scale


QA0, KA0, VA0 = 0, 1024, 2048
QB0, KB0, VB0, GB0 = 3072, 3584, 4096, 5120
B_QK_W = B_HEADS * B_QK_DIM


def _even_mixer(proj, e, l, diff_lambda, diff_norm_g, ret_decay_logit, ret_norm_g,
                cache_attn_k, cache_attn_v, state_retention, rope_a, rope_b):
    lam_init = 0.8 - 0.6 * math.exp(-0.3 * l)
    oa_ctx = _diff_attention((proj, QA0, 0), (proj, KA0, 0), (proj, VA0, 0), BATCH, SEQ,
                             diff_lambda, diff_norm_g, lam_init, tq=SEQ)
    qa_r = _rope(proj, QA0, MIX_HALF, T_CTX, T_LAT, *rope_a, half=A_QK_DIM // 2)
    ka_r = _rope(proj, KA0, MIX_HALF, T_CTX, T_LAT, *rope_a, half=A_QK_DIM // 2)
    oa_lat = _diff_attention((qa_r, 0, 0), (ka_r, 0, 0), (proj, VA0, T_CTX), DEC_BATCH, DEC_SEQ,
                             diff_lambda, diff_norm_g, lam_init,
                             ctx_k=cache_attn_k[:, e], ctx_v=cache_attn_v[:, e], tq=256)
    mix_a = jnp.concatenate([oa_ctx, oa_lat], axis=0)

    k_scale = B_QK_DIM ** -0.5
    qb_r = _rope(proj, QB0, B_QK_W, T_CTX, T_LAT, *rope_b, half=B_QK_DIM // 2)
    kb_r = _rope(proj, KB0, B_QK_W, T_CTX, T_LAT, *rope_b, half=B_QK_DIM // 2, scale=k_scale)

    def heads(a, b, L, hd):
        return a.astype(F32).reshape(b, L, B_HEADS, hd)

    zeros = jnp.zeros((BATCH, B_HEADS, B_QK_DIM, B_V_DIM), F32)
    pc = proj[:T_CTX]
    ob_ctx, sf, sb = _retention_jnp(
        heads(pc[:, QB0:KB0], BATCH, SEQ, B_QK_DIM), heads(pc[:, KB0:VB0], BATCH, SEQ, B_QK_DIM) * k_scale,
        heads(pc[:, VB0:GB0], BATCH, SEQ, B_V_DIM), pc[:, GB0:].reshape(BATCH, SEQ, MIX_HALF),
        ret_decay_logit, ret_norm_g, zeros, zeros)
    plat = proj[T_CTX:]
    ob_lat, _, _ = _retention_jnp(
        heads(qb_r, DEC_BATCH, DEC_SEQ, B_QK_DIM), heads(kb_r, DEC_BATCH, DEC_SEQ, B_QK_DIM),
        heads(plat[:, VB0:GB0], DEC_BATCH, DEC_SEQ, B_V_DIM), plat[:, GB0:].reshape(DEC_BATCH, DEC_SEQ, MIX_HALF),
        ret_decay_logit, ret_norm_g, state_retention[:, e, 0], state_retention[:, e, 1])
    mix_b = jnp.concatenate([ob_ctx, ob_lat], axis=0)

    ka = pc[:, KA0:VA0].reshape(BATCH, SEQ, A_HEADS, 2 * A_QK_DIM).transpose(0, 2, 1, 3)
    va = pc[:, VA0:QB0].reshape(BATCH, SEQ, A_HEADS, A_V_DIM).transpose(0, 2, 1, 3)
    return mix_a, mix_b, ka, va, jnp.stack([sf, sb], axis=1)


def _odd_mixer(proj, sg_g, sg_b, w_s, b_s, p_w, p_scale):
    outs_c, outs_d = [], []
    for r0, b, L in ((0, BATCH, SEQ), (T_CTX, DEC_BATCH, DEC_SEQ)):
        pr = proj[r0:r0 + b * L].reshape(b, L, ODD_IN)
        u, v, p = pr[..., :C_WIDTH], pr[..., C_WIDTH:2 * C_WIDTH], pr[..., 2 * C_WIDTH:]
        outs_c.append(_spatial_gating_jnp(u, v, sg_g, sg_b, w_s, b_s).reshape(b * L, C_WIDTH))
        outs_d.append(_multiscale_pool_jnp(p, p_w, p_scale).reshape(b * L, D_WIDTH))
    return jnp.concatenate(outs_c, axis=0).astype(BF16), jnp.concatenate(outs_d, axis=0).astype(BF16)


def kernel(x_prompt, x_sample, cache_attn_k, cache_attn_v, state_retention, c, c_ctx, w_ada, b_ada, ln_g, ln_b,
           w_in_even, diff_lambda, diff_norm_g, ret_decay_logit, ret_norm_g, w_in_odd, sgu_ln_g, sgu_ln_b,
           sgu_w, sgu_b, pool_w, pool_scale, w_out, peer_wq, peer_keys, peer_u, peer_v):
    x = jnp.concatenate([x_prompt.reshape(T_CTX, D_MODEL), x_sample.reshape(T_LAT, D_MODEL)], axis=0)
    cond = jnp.concatenate([c_ctx[None, :], c, jnp.zeros((COND_PAD - N_COND, D_MODEL), F32)], axis=0)
    mods = _ada(cond, w_ada, b_ada)[:, :N_COND].reshape(DEPTH, N_COND, 6, D_MODEL)

    rope_a = _rope_tables(DEC_SEQ, A_QK_DIM)
    rope_b = _rope_tables(DEC_SEQ, B_QK_DIM)
    w_in_even_b = w_in_even.astype(BF16)
    w_in_odd_b = w_in_odd.astype(BF16)
    w_out_b = w_out.astype(BF16)
    peer_u_b = peer_u.astype(BF16)
    peer_vt_b = peer_v.astype(BF16).transpose(0, 2, 1)

    ks, vs, ss = [], [], []
    for l in range(DEPTH):
        mods_l = mods[l]
        if l % 2 == 0:
            e = l // 2
            proj = _inproj(x, mods_l, w_in_even_b[e])
            mix_a, mix_b, ka, va, st = _even_mixer(
                proj, e, l, diff_lambda[e], diff_norm_g[e], ret_decay_logit[e], ret_norm_g[e],
                cache_attn_k, cache_attn_v, state_retention, rope_a, rope_b)
            ks.append(ka)
            vs.append(va)
            ss.append(st)
        else:
            o = l // 2
            proj = _inproj(x, mods_l, w_in_odd_b[o])
            mix_a, mix_b = _odd_mixer(proj, sgu_ln_g[o], sgu_ln_b[o], sgu_w[o], sgu_b[o], pool_w[o], pool_scale[o])
        x1, h2t = _outproj(mix_a, mix_b, x, mods_l, w_out_b[l], ln_g[l, 0][None, :], ln_b[l, 0][None, :])
        wqk_t = _wqk_fold(peer_keys[l], peer_wq[l])
        scores_t = _scores(wqk_t, h2t)
        r2t, e2t, rows_t = _peer_routing(scores_t)
        x = _peer_dense(h2t, peer_u_b[l], peer_vt_b[l], r2t, e2t, rows_t, x1, mods_l,
                        ln_g[l, 1][None, :], ln_b[l, 1][None, :])

    y_prompt = x[:T_CTX].reshape(BATCH, SEQ, D_MODEL)
    y_sample = x[T_CTX:].reshape(DEC_BATCH, DEC_SEQ, D_MODEL)
    return (y_prompt, y_sample, jnp.stack(ks, axis=1), jnp.stack(vs, axis=1), jnp.stack(ss, axis=1))
```

```python
import functools
import math

import jax
import jax.numpy as jnp
from jax import lax
from jax.experimental import pallas as pl
from jax.experimental.pallas import tpu as pltpu

F32 = jnp.float32
BF16 = jnp.bfloat16

D_MODEL = 2048
BATCH = 32
SEQ = 256
DEPTH = 4
DEC_BATCH = 2
DEC_SEQ = 4096
PAST_LEN = 256
GRID_W = 64
N_EVEN = (DEPTH + 1) // 2
MIX_HALF = D_MODEL // 2
A_HEADS = 8
A_QK_DIM = 64
A_V_DIM = MIX_HALF // A_HEADS
B_HEADS = 4
B_QK_DIM = 128
B_V_DIM = MIX_HALF // B_HEADS
RET_CHUNK = 128
C_WIDTH = MIX_HALF
C_GROUPS = 4
C_CHUNK = 128
D_WIDTH = MIX_HALF
POOL_WINDOWS = (2, 4, 8, 16)
N_POOL = 4
D_GROUP = D_WIDTH // N_POOL
PEER_HEADS = 8
PEER_NKEYS = 128
PEER_NEXP = PEER_NKEYS * PEER_NKEYS
PEER_QDIM = 256
PEER_TOPK = 16
ROPE_BASE = 10000.0
LN_EPS = 1e-5
DEEPNORM_ALPHA = (2 * DEPTH) ** 0.25
EVEN_IN = 6144
ODD_IN = 2 * C_WIDTH + D_WIDTH

T_CTX = BATCH * SEQ
T_LAT = DEC_BATCH * DEC_SEQ
T_ALL = T_CTX + T_LAT
N_COND = 1 + DEC_BATCH
COND_PAD = 8

LANES = 128
VMEM_LIMIT = 56 * 1024 * 1024


def _cparams(sem):
    return pltpu.CompilerParams(dimension_semantics=sem, vmem_limit_bytes=VMEM_LIMIT)


def _mod_row(i, tm):
    start = i * tm
    return jnp.where(start < T_CTX, 0, 1 + (start - T_CTX) // DEC_SEQ)


def _ada_kernel(cond_ref, w_ref, b_ref, o_ref):
    c = cond_ref[...]
    s = (c * jax.nn.sigmoid(c)).astype(BF16)
    o_ref[...] = jnp.dot(s, w_ref[...].astype(BF16), preferred_element_type=F32) + b_ref[...]


def _ada(cond, w_ada, b_ada, tn=1024):
    depth, d, n = w_ada.shape
    return pl.pallas_call(
        _ada_kernel,
        grid=(depth, n // tn),
        in_specs=[pl.BlockSpec((COND_PAD, d), lambda l, j: (0, 0)),
                  pl.BlockSpec((None, d, tn), lambda l, j: (l, 0, j)),
                  pl.BlockSpec((None, 1, tn), lambda l, j: (l, 0, j))],
        out_specs=pl.BlockSpec((None, COND_PAD, tn), lambda l, j: (l, 0, j)),
        out_shape=jax.ShapeDtypeStruct((depth, COND_PAD, n), F32),
        compiler_params=_cparams(("parallel", "parallel")),
        name="ada",
    )(cond, w_ada, b_ada.reshape(depth, 1, n))


def _inproj_kernel(x_ref, mod_ref, w_ref, o_ref, h_ref):
    @pl.when(pl.program_id(1) == 0)
    def _():
        sh = mod_ref[0:1, :]
        sc = mod_ref[1:2, :]
        h_ref[...] = (x_ref[...] * (1.0 + sc) + sh).astype(BF16)

    o_ref[...] = jnp.dot(h_ref[...], w_ref[...], preferred_element_type=F32)


def _inproj(x, mods_l, w, tm=1024, tn=1024):
    t, d = x.shape
    n = w.shape[1]
    return pl.pallas_call(
        _inproj_kernel,
        grid=(t // tm, n // tn),
        in_specs=[pl.BlockSpec((tm, d), lambda i, j: (i, 0)),
                  pl.BlockSpec((None, 6, d), lambda i, j: (_mod_row(i, tm), 0, 0)),
                  pl.BlockSpec((d, tn), lambda i, j: (0, j))],
        out_specs=pl.BlockSpec((tm, tn), lambda i, j: (i, j)),
        out_shape=jax.ShapeDtypeStruct((t, n), F32),
        scratch_shapes=[pltpu.VMEM((tm, d), BF16)],
        compiler_params=_cparams(("parallel", "arbitrary")),
        name="inproj",
    )(x, mods_l, w)


def _layer_norm_rows(z, g, b):
    mu = jnp.mean(z, axis=-1, keepdims=True)
    zc = z - mu
    var = jnp.mean(zc * zc, axis=-1, keepdims=True)
    return zc * lax.rsqrt(var + LN_EPS) * g + b


def _outproj_kernel(ma_ref, mb_ref, x_ref, mod_ref, wa_ref, wb_ref, g_ref, b_ref, x1_ref, h2t_ref):
    o = jnp.dot(ma_ref[...], wa_ref[...], preferred_element_type=F32)
    o = o + jnp.dot(mb_ref[...], wb_ref[...], preferred_element_type=F32)
    g1 = mod_ref[2:3, :]
    x1 = _layer_norm_rows(DEEPNORM_ALPHA * x_ref[...] + g1 * o, g_ref[...], b_ref[...])
    x1_ref[...] = x1
    h2 = x1 * (1.0 + mod_ref[4:5, :]) + mod_ref[3:4, :]
    h2t_ref[...] = h2.T.astype(BF16)


def _outproj(mix_a, mix_b, x, mods_l, w_out, ln_g, ln_b, tm=512):
    t, d = x.shape
    ka = mix_a.shape[1]
    kb = mix_b.shape[1]
    return pl.pallas_call(
        _outproj_kernel,
        grid=(t // tm,),
        in_specs=[pl.BlockSpec((tm, ka), lambda i: (i, 0)),
                  pl.BlockSpec((tm, kb), lambda i: (i, 0)),
                  pl.BlockSpec((tm, d), lambda i: (i, 0)),
                  pl.BlockSpec((None, 6, d), lambda i: (_mod_row(i, tm), 0, 0)),
                  pl.BlockSpec((ka, d), lambda i: (0, 0)),
                  pl.BlockSpec((kb, d), lambda i: (1, 0)),
                  pl.BlockSpec((1, d), lambda i: (0, 0)),
                  pl.BlockSpec((1, d), lambda i: (0, 0))],
        out_specs=[pl.BlockSpec((tm, d), lambda i: (i, 0)),
                   pl.BlockSpec((d, tm), lambda i: (0, i))],
        out_shape=[jax.ShapeDtypeStruct((t, d), F32),
                   jax.ShapeDtypeStruct((d, t), BF16)],
        compiler_params=_cparams(("parallel",)),
        name="outproj_ln",
    )(mix_a, mix_b, x, mods_l, w_out, w_out, ln_g, ln_b)


def _split_bf16(x):
    hi = x.astype(BF16)
    lo = (x - hi.astype(F32)).astype(BF16)
    return hi, lo


def _wqk_kernel(k_ref, wq_ref, o_ref):
    kh, kl = _split_bf16(k_ref[...])
    wh, wl = _split_bf16(wq_ref[...])
    nt = (((1,), (1,)), ((), ()))
    acc = lax.dot_general(kh, wh, nt, preferred_element_type=F32)
    acc = acc + lax.dot_general(kh, wl, nt, preferred_element_type=F32)
    acc = acc + lax.dot_general(kl, wh, nt, preferred_element_type=F32)
    o_ref[...] = acc.astype(BF16)


def _wqk_fold(keys_l, wq_l):
    h, p, nk, half = keys_l.shape
    d = wq_l.shape[0]
    return pl.pallas_call(
        _wqk_kernel,
        grid=(h * p,),
        in_specs=[pl.BlockSpec((None, nk, half), lambda i: (i, 0, 0)),
                  pl.BlockSpec((d, half), lambda i: (0, i))],
        out_specs=pl.BlockSpec((nk, d), lambda i: (i, 0)),
        out_shape=jax.ShapeDtypeStruct((h * p * nk, d), BF16),
        compiler_params=_cparams(("parallel",)),
        name="peer_wqk_fold",
    )(keys_l.reshape(h * p, nk, half), wq_l)


def _scores_kernel(w_ref, h_ref, o_ref):
    o_ref[...] = jnp.dot(w_ref[...], h_ref[...], preferred_element_type=F32)


def _scores(wqk_t, h2t, tn=512):
    n, d = wqk_t.shape
    t = h2t.shape[1]
    return pl.pallas_call(
        _scores_kernel,
        grid=(t // tn,),
        in_specs=[pl.BlockSpec((n, d), lambda i: (0, 0)),
                  pl.BlockSpec((d, tn), lambda i: (0, i))],
        out_specs=pl.BlockSpec((n, tn), lambda i: (0, i)),
        out_shape=jax.ShapeDtypeStruct((n, t), F32),
        compiler_params=_cparams(("parallel",)),
        name="peer_scores",
    )(wqk_t, h2t)


def _gelu_tanh(x):
    c = math.sqrt(2.0 / math.pi)
    return 0.5 * x * (1.0 + jnp.tanh(c * (x + 0.044715 * (x * x * x))))


def _peer_kernel(h2t_ref, u_ref, vt_ref, r2_ref, e2_ref, lim_ref, w1_ref, x1_ref, mod_ref, g_ref, b_ref,
                 o_ref, acc_ref, w_ref, *, n_i, n_heads, nk):
    eb = pl.program_id(1)

    @pl.when(eb == 0)
    def _():
        acc_ref[...] = jnp.zeros_like(acc_ref)

    a = jnp.dot(u_ref[...], h2t_ref[...], preferred_element_type=F32)
    for ii in range(n_i):
        gate = None
        for h in range(n_heads):
            lim = lim_ref[h, ii:ii + 1, :].astype(BF16)
            w1 = w1_ref[h, ii:ii + 1, :].astype(BF16)
            term = jnp.where(r2_ref[h] < lim, e2_ref[h], jnp.zeros((), BF16)) * w1
            gate = term if gate is None else gate + term
        act = _gelu_tanh(a[ii * nk:(ii + 1) * nk, :]).astype(BF16)
        w_ref[ii * nk:(ii + 1) * nk, :] = act * gate
    acc_ref[...] += jnp.dot(vt_ref[...], w_ref[...], preferred_element_type=F32)

    @pl.when(eb == pl.num_programs(1) - 1)
    def _():
        g2 = mod_ref[5:6, :]
        tm = acc_ref.shape[1]
        for c in range(tm // LANES):
            rows = slice(c * LANES, (c + 1) * LANES)
            y = acc_ref[:, rows].T
            z = DEEPNORM_ALPHA * x1_ref[rows, :] + g2 * y
            o_ref[rows, :] = _layer_norm_rows(z, g_ref[...], b_ref[...])


def _peer_dense(h2t, u, vt, r2t, e2t, lim1t, w1t, x1, mods_l, ln_g, ln_b, tm=512, n_i=8):
    d, t = h2t.shape
    nexp = u.shape[0]
    n_heads, nk, _ = r2t.shape
    te = n_i * nk
    kern = functools.partial(_peer_kernel, n_i=n_i, n_heads=n_heads, nk=nk)
    return pl.pallas_call(
        kern,
        grid=(t // tm, nexp // te),
        in_specs=[pl.BlockSpec((d, tm), lambda i, e: (0, i)),
                  pl.BlockSpec((te, d), lambda i, e: (e, 0)),
                  pl.BlockSpec((d, te), lambda i, e: (0, e)),
                  pl.BlockSpec((n_heads, nk, tm), lambda i, e: (0, 0, i)),
                  pl.BlockSpec((n_heads, nk, tm), lambda i, e: (0, 0, i)),
                  pl.BlockSpec((n_heads, n_i, tm), lambda i, e: (0, e, i)),
                  pl.BlockSpec((n_heads, n_i, tm), lambda i, e: (0, e, i)),
                  pl.BlockSpec((tm, d), lambda i, e: (i, 0)),
                  pl.BlockSpec((None, 6, d), lambda i, e: (_mod_row(i, tm), 0, 0)),
                  pl.BlockSpec((1, d), lambda i, e: (0, 0)),
                  pl.BlockSpec((1, d), lambda i, e: (0, 0))],
        out_specs=pl.BlockSpec((tm, d), lambda i, e: (i, 0)),
        out_shape=jax.ShapeDtypeStruct((t, d), F32),
        scratch_shapes=[pltpu.VMEM((d, tm), F32), pltpu.VMEM((te, tm), BF16)],
        compiler_params=_cparams(("parallel", "arbitrary")),
        name="peer_dense",
    )(h2t, u, vt, r2t, e2t, lim1t, w1t, x1, mods_l, ln_g, ln_b)


def _topk_rows(s, k):
    n, lanes = s.shape
    row = lax.broadcasted_iota(jnp.int32, s.shape, 0).astype(F32)
    row_k = lax.broadcasted_iota(jnp.int32, (k, lanes), 0)
    rank = jnp.full(s.shape, float(k), F32)
    vals = jnp.zeros((k, lanes), F32)
    for r in range(k):
        m = jnp.max(s, axis=0, keepdims=True)
        first = jnp.min(jnp.where(s == m, row, float(n)), axis=0, keepdims=True)
        sel = row == first
        rank = jnp.where(sel, float(r), rank)
        vals = jnp.where(row_k == r, m, vals)
        s = jnp.where(sel, -jnp.inf, s)
    return rank, vals


_CAND_ROWS = 16 + 7 * 8 + 8


def _routing_kernel(s_ref, r2_ref, e2_ref, lim_ref, w1_ref, *, n_heads, nk, n_sub):
    k = PEER_TOPK

    def body(it, carry):
        h = it // n_sub
        lanes = pl.ds(pl.multiple_of((it % n_sub) * LANES, LANES), LANES)
        s1 = s_ref[pl.ds(pl.multiple_of(h * 2 * nk, nk), nk), lanes]
        s2 = s_ref[pl.ds(pl.multiple_of(h * 2 * nk + nk, nk), nk), lanes]
        rank1, v1 = _topk_rows(s1, k)
        rank2, v2 = _topk_rows(s2, k)

        groups = [v1[0:1, :] + v2] + [v1[a:a + 1, :] + v2[0:8, :] for a in range(1, 8)] + [v1[8:16, :] + v2[0:1, :]]
        cand = jnp.concatenate(groups, axis=0)
        r = lax.broadcasted_iota(jnp.int32, cand.shape, 0)
        flat = jnp.where(r < 16, r, jnp.where(r < 72, (1 + (r - 16) // 8) * k + (r - 16) % 8, (r - 64) * k)).astype(F32)
        top = v1[0:1, :] + v2[0:1, :]
        picked = jnp.zeros(cand.shape, F32)
        z = jnp.zeros((1, LANES), F32)
        for _ in range(k):
            m = jnp.max(cand, axis=0, keepdims=True)
            first = jnp.min(jnp.where(cand == m, flat, float(k * k)), axis=0, keepdims=True)
            sel = flat == first
            picked = jnp.where(sel, 1.0, picked)
            z = z + jnp.exp(m - top)
            cand = jnp.where(sel, -jnp.inf, cand)

        n_rows = [jnp.sum(picked[0:16, :], axis=0, keepdims=True)]
        n_rows += [jnp.sum(picked[16 + 8 * (a - 1):16 + 8 * a, :], axis=0, keepdims=True) for a in range(1, 8)]
        n_rows += [picked[64 + a:65 + a, :] for a in range(8, 16)]
        lim = jnp.zeros(s1.shape, F32)
        for a in range(k):
            lim = jnp.where(rank1 == float(a), n_rows[a], lim)

        r2_ref[h, :, lanes] = rank2.astype(BF16)
        e2_ref[h, :, lanes] = jnp.exp(s2 - v2[0:1, :]).astype(BF16)
        lim_ref[h, :, lanes] = lim
        w1_ref[h, :, lanes] = jnp.exp(s1 - v1[0:1, :]) * (1.0 / z)
        return carry

    lax.fori_loop(0, n_heads * n_sub, body, 0)


def _peer_routing(scores_t, tl=512):
    assert PEER_TOPK == 16 and _CAND_ROWS == 80
    n, t = scores_t.shape
    nh, nk = PEER_HEADS, PEER_NKEYS
    kern = functools.partial(_routing_kernel, n_heads=nh, nk=nk, n_sub=tl // LANES)
    spec = pl.BlockSpec((nh, nk, tl), lambda i: (0, 0, i))
    return pl.pallas_call(
        kern,
        grid=(t // tl,),
        in_specs=[pl.BlockSpec((n, tl), lambda i: (0, i))],
        out_specs=[spec, spec, spec, spec],
        out_shape=[jax.ShapeDtypeStruct((nh, nk, t), BF16), jax.ShapeDtypeStruct((nh, nk, t), BF16),
                   jax.ShapeDtypeStruct((nh, nk, t), F32), jax.ShapeDtypeStruct((nh, nk, t), F32)],
        compiler_params=_cparams(("parallel",)),
        name="peer_routing",
    )(scores_t)


def _rope_tables(n_tokens, dim):
    rows = n_tokens // GRID_W
    row = jnp.repeat(jnp.arange(rows, dtype=F32), GRID_W)
    col = jnp.tile(jnp.arange(GRID_W, dtype=F32), rows)
    quarter = dim // 4
    freqs = ROPE_BASE ** (-jnp.arange(quarter, dtype=F32) / quarter)
    ang = jnp.concatenate([row[:, None] * freqs, col[:, None] * freqs], axis=-1)
    cos, sin = jnp.cos(ang), jnp.sin(ang)
    reps = LANES // dim
    cos_full = jnp.tile(jnp.concatenate([cos, cos], axis=-1), (1, reps))
    sin_full = jnp.tile(jnp.concatenate([-sin, sin], axis=-1), (1, reps))
    return cos_full, sin_full


def _rope_kernel(x_ref, cos_ref, sin_ref, o_ref, *, half, scale):
    x = x_ref[...]
    w = x.shape[1]
    lane = lax.broadcasted_iota(jnp.int32, x.shape, 1)
    partner = jnp.where((lane % (2 * half)) < half, pltpu.roll(x, w - half, 1), pltpu.roll(x, half, 1))
    o_ref[...] = ((x * cos_ref[...] + partner * sin_ref[...]) * scale).astype(o_ref.dtype)


def _rope(proj, col0, ncols, row0, nrows, cos_full, sin_full, half, scale=1.0, tm=512):
    kern = functools.partial(_rope_kernel, half=half, scale=scale)
    cb0 = col0 // LANES
    rb0 = row0 // tm
    return pl.pallas_call(
        kern,
        grid=(nrows // tm, ncols // LANES),
        in_specs=[pl.BlockSpec((tm, LANES), lambda i, j: (rb0 + i, cb0 + j)),
                  pl.BlockSpec((tm, LANES), lambda i, j: (i % (cos_full.shape[0] // tm), 0)),
                  pl.BlockSpec((tm, LANES), lambda i, j: (i % (cos_full.shape[0] // tm), 0))],
        out_specs=pl.BlockSpec((tm, LANES), lambda i, j: (i, j)),
        out_shape=jax.ShapeDtypeStruct((nrows, ncols), BF16),
        compiler_params=_cparams(("parallel", "parallel")),
        name="rope",
    )(proj, cos_full, sin_full)


def _attn_kernel(*refs, tq, l_self, l_ctx, lam_init):
    if l_ctx:
        lam_ref, q_ref, k_ref, v_ref, ck_ref, cv_ref, g_ref, o_ref, ks_ref, vs_ref = refs
    else:
        lam_ref, q_ref, k_ref, v_ref, g_ref, o_ref, ks_ref, vs_ref = refs

    @pl.when(pl.program_id(2) == 0)
    def _():
        if l_ctx:
            ks_ref[0:l_ctx, :] = ck_ref[...].astype(BF16)
            vs_ref[0:l_ctx, :] = cv_ref[...].astype(BF16)
        ks_ref[l_ctx:l_ctx + l_self, :] = k_ref[...].astype(BF16)
        vs_ref[l_ctx:l_ctx + l_self, :] = v_ref[...].astype(BF16)

    lv = lam_ref[...]
    lam = (jnp.exp(jnp.sum(lv[0:1, :] * lv[1:2, :], axis=-1, keepdims=True))
           - jnp.exp(jnp.sum(lv[2:3, :] * lv[3:4, :], axis=-1, keepdims=True)) + lam_init)
    q = q_ref[...].astype(F32) * (A_QK_DIM ** -0.5)
    lane = lax.broadcasted_iota(jnp.int32, q.shape, 1)
    first = lane < A_QK_DIM
    q0 = jnp.where(first, q, 0.0).astype(BF16)
    q1 = jnp.where(first, 0.0, q).astype(BF16)
    qs = jnp.concatenate([q0, q1], axis=0)
    s = lax.dot_general(qs, ks_ref[...], (((1,), (1,)), ((), ())), preferred_element_type=F32)
    m = jnp.max(s, axis=-1, keepdims=True)
    p = jnp.exp(s - m)
    p = p / jnp.sum(p, axis=-1, keepdims=True)
    w = (p[0:tq, :] - lam * p[tq:2 * tq, :]).astype(BF16)
    o = jnp.dot(w, vs_ref[...], preferred_element_type=F32)
    o = o * lax.rsqrt(jnp.mean(o * o, axis=-1, keepdims=True) + LN_EPS) * g_ref[...] * (1.0 - lam_init)
    o_ref[...] = o.astype(o_ref.dtype)


def _diff_attention(q_src, k_src, v_src, nb, l_self, lam_vec, norm_g, lam_init, ctx_k=None, ctx_v=None, tq=256):
    (q_arr, q_col0, q_row0), (k_arr, k_col0, k_row0), (v_arr, v_col0, v_row0) = q_src, k_src, v_src
    l_ctx = 0 if ctx_k is None else ctx_k.shape[2]
    nq = l_self // tq
    qb0, kb0, vb0 = q_col0 // LANES, k_col0 // LANES, v_col0 // LANES
    rq0, rk0, rv0 = q_row0 // tq, k_row0 // l_self, v_row0 // l_self
    kern = functools.partial(_attn_kernel, tq=tq, l_self=l_self, l_ctx=l_ctx, lam_init=lam_init)
    in_specs = [pl.BlockSpec((4, A_QK_DIM), lambda b, h, i: (0, 0)),
                pl.BlockSpec((tq, LANES), lambda b, h, i: (rq0 + b * nq + i, qb0 + h)),
                pl.BlockSpec((l_self, LANES), lambda b, h, i: (rk0 + b, kb0 + h)),
                pl.BlockSpec((l_self, LANES), lambda b, h, i: (rv0 + b, vb0 + h))]
    args = [lam_vec, q_arr, k_arr, v_arr]
    if l_ctx:
        in_specs += [pl.BlockSpec((None, None, l_ctx, LANES), lambda b, h, i: (b, h, 0, 0)),
                     pl.BlockSpec((None, None, l_ctx, LANES), lambda b, h, i: (b, h, 0, 0))]
        args += [ctx_k, ctx_v]
    in_specs += [pl.BlockSpec((None, 1, LANES), lambda b, h, i: (h, 0, 0))]
    args += [norm_g.reshape(A_HEADS, 1, A_V_DIM)]
    return pl.pallas_call(
        kern,
        grid=(nb, A_HEADS, nq),
        in_specs=in_specs,
        out_specs=pl.BlockSpec((tq, LANES), lambda b, h, i: (b * nq + i, h)),
        out_shape=jax.ShapeDtypeStruct((nb * l_self, A_HEADS * A_V_DIM), BF16),
        scratch_shapes=[pltpu.VMEM((l_ctx + l_self, LANES), BF16),
                        pltpu.VMEM((l_ctx + l_self, LANES), BF16)],
        compiler_params=_cparams(("parallel", "parallel", "arbitrary")),
        name="diff_attention",
    )(*args)


def _ret_kernel(*refs, l_self, has_state, k_scale):
    if has_state:
        dec_ref, q_ref, k_ref, v_ref, g_ref, ng_ref, s0_ref, o_ref, st_ref, of_ref = refs
    else:
        dec_ref, q_ref, k_ref, v_ref, g_ref, ng_ref, o_ref, st_ref, of_ref = refs
    c = RET_CHUNK
    nc = l_self // c
    lg = jax.nn.log_sigmoid(dec_ref[...])
    pos_i = lax.broadcasted_iota(jnp.int32, (c, c), 0).astype(F32)
    pos_j = lax.broadcasted_iota(jnp.int32, (c, c), 1).astype(F32)
    col = lax.broadcasted_iota(jnp.int32, (c, 1), 0).astype(F32)
    nt = (((1,), (1,)), ((), ()))

    def direction(d):
        lg_row = lg[d:d + 1, :]
        lg1 = lg_row[:, 0:1]
        dist = (pos_i - pos_j) if d == 0 else (pos_j - pos_i)
        intra = jnp.where(dist >= 0, jnp.exp(lg_row * jnp.maximum(dist, 0.0)), 0.0)
        q_dec = jnp.exp(lg1 * (col + 1.0)) if d == 0 else jnp.exp(lg1 * (c - col))
        k_dec = jnp.exp(lg1 * (c - 1.0 - col)) if d == 0 else jnp.exp(lg1 * col)
        chunk_dec = jnp.exp(lg1 * float(c))
        s_init = s0_ref[d].astype(F32) if has_state else jnp.zeros((B_QK_DIM, B_V_DIM), F32)

        def step(it, s):
            ci = it if d == 0 else nc - 1 - it
            rows = pl.ds(pl.multiple_of(ci * c, c), c)
            qc = q_ref[rows, :].astype(BF16)
            kf = k_ref[rows, :].astype(F32) * k_scale
            vc = v_ref[rows, :].astype(BF16)
            sc = lax.dot_general(qc, kf.astype(BF16), nt, preferred_element_type=F32) * intra
            o = jnp.dot(sc.astype(BF16), vc, preferred_element_type=F32)
            o = o + jnp.dot(qc, s.astype(BF16), preferred_element_type=F32) * q_dec
            if d == 0:
                of_ref[rows, :] = o
            else:
                o = o + of_ref[rows, :]
                mu = jnp.mean(o, axis=-1, keepdims=True)
                oc = o - mu
                var = jnp.mean(oc * oc, axis=-1, keepdims=True)
                gate = g_ref[rows, :]
                gate = gate * jax.nn.sigmoid(gate)
                o_ref[rows, :] = (oc * lax.rsqrt(var + LN_EPS) * ng_ref[...] * gate).astype(o_ref.dtype)
            kd = (kf * k_dec).T.astype(BF16)
            return s * chunk_dec + jnp.dot(kd, vc, preferred_element_type=F32)

        st_ref[d] = lax.fori_loop(0, nc, step, s_init)

    direction(0)
    direction(1)


def _retention(q_src, k_src, v_src, g_src, nb, l_self, decay_logit, norm_g, k_scale, s0=None):
    (q_arr, q_col0, q_row0), (k_arr, k_col0, k_row0) = q_src, k_src
    (v_arr, v_col0, v_row0), (g_arr, g_col0, g_row0) = v_src, g_src
    has_state = s0 is not None
    kern = functools.partial(_ret_kernel, l_self=l_self, has_state=has_state, k_scale=k_scale)
    dec = jnp.broadcast_to(decay_logit.T[:, :, None], (B_HEADS, 2, LANES)).astype(F32)
    qc0, kc0 = q_col0 // B_QK_DIM, k_col0 // B_QK_DIM
    vc0, gc0 = v_col0 // B_V_DIM, g_col0 // B_V_DIM
    rq, rk, rv, rg = q_row0 // l_self, k_row0 // l_self, v_row0 // l_self, g_row0 // l_self
    in_specs = [pl.BlockSpec((None, 2, LANES), lambda b, h: (h, 0, 0)),
                pl.BlockSpec((l_self, B_QK_DIM), lambda b, h: (rq + b, qc0 + h)),
                pl.BlockSpec((l_self, B_QK_DIM), lambda b, h: (rk + b, kc0 + h)),
                pl.BlockSpec((l_self, B_V_DIM), lambda b, h: (rv + b, vc0 + h)),
                pl.BlockSpec((l_self, B_V_DIM), lambda b, h: (rg + b, gc0 + h)),
                pl.BlockSpec((None, 1, B_V_DIM), lambda b, h: (h, 0, 0))]
    args = [dec, q_arr, k_arr, v_arr, g_arr, norm_g.reshape(B_HEADS, 1, B_V_DIM)]
    state_spec = pl.BlockSpec((None, 2, None, B_QK_DIM, B_V_DIM), lambda b, h: (b, 0, h, 0, 0))
    if has_state:
        in_specs.append(state_spec)
        args.append(s0)
    return pl.pallas_call(
        kern,
        grid=(nb, B_HEADS),
        in_specs=in_specs,
        out_specs=[pl.BlockSpec((l_self, B_V_DIM), lambda b, h: (b, h)), state_spec],
        out_shape=[jax.ShapeDtypeStruct((nb * l_self, B_HEADS * B_V_DIM), BF16),
                   jax.ShapeDtypeStruct((nb, 2, B_HEADS, B_QK_DIM, B_V_DIM), F32)],
        scratch_shapes=[pltpu.VMEM((l_self, B_V_DIM), F32)],
        compiler_params=_cparams(("parallel", "parallel")),
        name="retention",
    )(*args)


def _sgu_kernel(u_ref, v_ref, g_ref, b_ref, ws_ref, bs_ref, o_ref):
    vn = _layer_norm_rows(v_ref[...], g_ref[...], b_ref[...]).astype(BF16)
    tm = vn.shape[0]
    gw = C_WIDTH // C_GROUPS
    for gi in range(C_GROUPS):
        w = ws_ref[gi].astype(BF16)
        bias = bs_ref[:, gi:gi + 1]
        cols = slice(gi * gw, (gi + 1) * gw)
        for ci in range(tm // C_CHUNK):
            rows = slice(ci * C_CHUNK, (ci + 1) * C_CHUNK)
            mixed = jnp.dot(w, vn[rows, cols], preferred_element_type=F32) + bias
            o_ref[rows, cols] = (u_ref[rows, cols] * mixed).astype(o_ref.dtype)


def _sgu(proj, sg_g, sg_b, w_s, b_s, tm=512):
    t = proj.shape[0]
    return pl.pallas_call(
        _sgu_kernel,
        grid=(t // tm,),
        in_specs=[pl.BlockSpec((tm, C_WIDTH), lambda i: (i, 0)),
                  pl.BlockSpec((tm, C_WIDTH), lambda i: (i, 1)),
                  pl.BlockSpec((1, C_WIDTH), lambda i: (0, 0)),
                  pl.BlockSpec((1, C_WIDTH), lambda i: (0, 0)),
                  pl.BlockSpec((C_GROUPS, C_CHUNK, C_CHUNK), lambda i: (0, 0, 0)),
                  pl.BlockSpec((C_CHUNK, C_GROUPS), lambda i: (0, 0))],
        out_specs=pl.BlockSpec((tm, C_WIDTH), lambda i: (i, 0)),
        out_shape=jax.ShapeDtypeStruct((t, C_WIDTH), BF16),
        compiler_params=_cparams(("parallel",)),
        name="sgu",
    )(proj, proj, sg_g[None, :], sg_b[None, :], w_s, b_s.T)


def _pool_kernel(p_ref, w_ref, sc_ref, o_ref, *, l_seq):
    p = p_ref[...]
    n = p.shape[0]
    pos = (lax.broadcasted_iota(jnp.int32, p.shape, 0) % l_seq)
    w_mat = w_ref[...].astype(BF16)
    for gi, win in enumerate(POOL_WINDOWS):
        @pl.when(pl.program_id(1) == gi)
        def _(win=win):
            half = win // 2
            acc = jnp.zeros(p.shape, F32)
            for d in range(-half, half):
                shifted = p if d == 0 else pltpu.roll(p, (-d) % n, 0)
                ok = (pos + d >= 0) & (pos + d <= l_seq - 1)
                acc = acc + jnp.where(ok, shifted, 0.0)
            cnt = jnp.minimum(pos + (half - 1), l_seq - 1) - jnp.maximum(pos - half, 0) + 1
            pooled = acc / cnt.astype(F32) - p
            mixed = jnp.dot(pooled.astype(BF16), w_mat, preferred_element_type=F32)
            o_ref[...] = (mixed * sc_ref[...]).astype(o_ref.dtype)


def _pool(proj, col0, row0, nrows, l_seq, p_w, p_scale, rows_per_step):
    kern = functools.partial(_pool_kernel, l_seq=l_seq)
    cb0 = col0 // D_GROUP
    rb0 = row0 // rows_per_step
    return pl.pallas_call(
        kern,
        grid=(nrows // rows_per_step, N_POOL),
        in_specs=[pl.BlockSpec((rows_per_step, D_GROUP), lambda i, g: (rb0 + i, cb0 + g)),
                  pl.BlockSpec((None, D_GROUP, D_GROUP), lambda i, g: (g, 0, 0)),
                  pl.BlockSpec((1, D_GROUP), lambda i, g: (0, g))],
        out_specs=pl.BlockSpec((rows_per_step, D_GROUP), lambda i, g: (i, g)),
        out_shape=jax.ShapeDtypeStruct((nrows, D_WIDTH), BF16),
        compiler_params=_cparams(("parallel", "parallel")),
        name="pool",
    )(proj, p_w, p_scale[None, :])


QA0, KA0, VA0 = 0, 1024, 2048
QB0, KB0, VB0, GB0 = 3072, 3584, 4096, 5120
B_QK_W = B_HEADS * B_QK_DIM


def _even_mixer(proj, e, l, diff_lambda, diff_norm_g, ret_decay_logit, ret_norm_g,
                cache_attn_k, cache_attn_v, state_retention, rope_a, rope_b):
    lam_init = 0.8 - 0.6 * math.exp(-0.3 * l)
    oa_ctx = _diff_attention((proj, QA0, 0), (proj, KA0, 0), (proj, VA0, 0), BATCH, SEQ,
                             diff_lambda, diff_norm_g, lam_init, tq=SEQ)
    qa_r = _rope(proj, QA0, MIX_HALF, T_CTX, T_LAT, *rope_a, half=A_QK_DIM // 2)
    ka_r = _rope(proj, KA0, MIX_HALF, T_CTX, T_LAT, *rope_a, half=A_QK_DIM // 2)
    oa_lat = _diff_attention((qa_r, 0, 0), (ka_r, 0, 0), (proj, VA0, T_CTX), DEC_BATCH, DEC_SEQ,
                             diff_lambda, diff_norm_g, lam_init,
                             ctx_k=cache_attn_k[:, e], ctx_v=cache_attn_v[:, e], tq=256)
    mix_a = jnp.concatenate([oa_ctx, oa_lat], axis=0)

    k_scale = B_QK_DIM ** -0.5
    qb_r = _rope(proj, QB0, B_QK_W, T_CTX, T_LAT, *rope_b, half=B_QK_DIM // 2)
    kb_r = _rope(proj, KB0, B_QK_W, T_CTX, T_LAT, *rope_b, half=B_QK_DIM // 2, scale=k_scale)

    ob_ctx, st_ctx = _retention((proj, QB0, 0), (proj, KB0, 0), (proj, VB0, 0), (proj, GB0, 0), BATCH, SEQ,
                                ret_decay_logit, ret_norm_g, k_scale)
    ob_lat, _ = _retention((qb_r, 0, 0), (kb_r, 0, 0), (proj, VB0, T_CTX), (proj, GB0, T_CTX), DEC_BATCH, DEC_SEQ,
                           ret_decay_logit, ret_norm_g, 1.0, s0=state_retention[:, e])
    mix_b = jnp.concatenate([ob_ctx, ob_lat], axis=0)

    pc = proj[:T_CTX]
    ka = pc[:, KA0:VA0].reshape(BATCH, SEQ, A_HEADS, 2 * A_QK_DIM).transpose(0, 2, 1, 3)
    va = pc[:, VA0:QB0].reshape(BATCH, SEQ, A_HEADS, A_V_DIM).transpose(0, 2, 1, 3)
    return mix_a, mix_b, ka, va, st_ctx


def _odd_mixer(proj, sg_g, sg_b, w_s, b_s, p_w, p_scale):
    mix_c = _sgu(proj, sg_g, sg_b, w_s, b_s)
    pool_ctx = _pool(proj, 2 * C_WIDTH, 0, T_CTX, SEQ, p_w, p_scale, rows_per_step=8 * SEQ)
    pool_lat = _pool(proj, 2 * C_WIDTH, T_CTX, T_LAT, DEC_SEQ, p_w, p_scale, rows_per_step=DEC_SEQ)
    return mix_c, jnp.concatenate([pool_ctx, pool_lat], axis=0)


def kernel(x_prompt, x_sample, cache_attn_k, cache_attn_v, state_retention, c, c_ctx, w_ada, b_ada, ln_g, ln_b,
           w_in_even, diff_lambda, diff_norm_g, ret_decay_logit, ret_norm_g, w_in_odd, sgu_ln_g, sgu_ln_b,
           sgu_w, sgu_b, pool_w, pool_scale, w_out, peer_wq, peer_keys, peer_u, peer_v):
    x = jnp.concatenate([x_prompt.reshape(T_CTX, D_MODEL), x_sample.reshape(T_LAT, D_MODEL)], axis=0)
    cond = jnp.concatenate([c_ctx[None, :], c, jnp.zeros((COND_PAD - N_COND, D_MODEL), F32)], axis=0)
    mods = _ada(cond, w_ada, b_ada)[:, :N_COND].reshape(DEPTH, N_COND, 6, D_MODEL)

    rope_a = _rope_tables(DEC_SEQ, A_QK_DIM)
    rope_b = _rope_tables(DEC_SEQ, B_QK_DIM)
    w_in_even_b = w_in_even.astype(BF16)
    w_in_odd_b = w_in_odd.astype(BF16)
    w_out_b = w_out.astype(BF16)
    peer_u_b = peer_u.astype(BF16)
    peer_vt_b = peer_v.astype(BF16).transpose(0, 2, 1)

    ks, vs, ss = [], [], []
    for l in range(DEPTH):
        mods_l = mods[l]
        if l % 2 == 0:
            e = l // 2
            proj = _inproj(x, mods_l, w_in_even_b[e])
            mix_a, mix_b, ka, va, st = _even_mixer(
                proj, e, l, diff_lambda[e], diff_norm_g[e], ret_decay_logit[e], ret_norm_g[e],
                cache_attn_k, cache_attn_v, state_retention, rope_a, rope_b)
            ks.append(ka)
            vs.append(va)
            ss.append(st)
        else:
            o = l // 2
            proj = _inproj(x, mods_l, w_in_odd_b[o])
            mix_a, mix_b = _odd_mixer(proj, sgu_ln_g[o], sgu_ln_b[o], sgu_w[o], sgu_b[o], pool_w[o], pool_scale[o])
        x1, h2t = _outproj(mix_a, mix_b, x, mods_l, w_out_b[l], ln_g[l, 0][None, :], ln_b[l, 0][None, :])
        wqk_t = _wqk_fold(peer_keys[l], peer_wq[l])
        scores_t = _scores(wqk_t, h2t)
        r2t, e2t, lim1t, w1t = _peer_routing(scores_t)
        x = _peer_dense(h2t, peer_u_b[l], peer_vt_b[l], r2t, e2t, lim1t, w1t, x1, mods_l,
                        ln_g[l, 1][None, :], ln_b[l, 1][None, :])

    y_prompt = x[:T_CTX].reshape(BATCH, SEQ, D_MODEL)
    y_sample = x[T_CTX:].reshape(DEC_BATCH, DEC_SEQ, D_MODEL)
    return (y_prompt, y_sample, jnp.stack(ks, axis=1), jnp.stack(vs, axis=1), jnp.stack(ss, axis=1))
```

```python
import functools
import math

import jax
import jax.numpy as jnp
from jax import lax
from jax.experimental import pallas as pl
from jax.experimental.pallas import tpu as pltpu

F32 = jnp.float32
BF16 = jnp.bfloat16

D_MODEL = 2048
BATCH = 32
SEQ = 256
DEPTH = 4
DEC_BATCH = 2
DEC_SEQ = 4096
PAST_LEN = 256
GRID_W = 64
N_EVEN = (DEPTH + 1) // 2
MIX_HALF = D_MODEL // 2
A_HEADS = 8
A_QK_DIM = 64
A_V_DIM = MIX_HALF // A_HEADS
B_HEADS = 4
B_QK_DIM = 128
B_V_DIM = MIX_HALF // B_HEADS
RET_CHUNK = 128
C_WIDTH = MIX_HALF
C_GROUPS = 4
C_CHUNK = 128
D_WIDTH = MIX_HALF
POOL_WINDOWS = (2, 4, 8, 16)
N_POOL = 4
D_GROUP = D_WIDTH // N_POOL
PEER_HEADS = 8
PEER_NKEYS = 128
PEER_NEXP = PEER_NKEYS * PEER_NKEYS
PEER_QDIM = 256
PEER_TOPK = 16
ROPE_BASE = 10000.0
LN_EPS = 1e-5
DEEPNORM_ALPHA = (2 * DEPTH) ** 0.25
EVEN_IN = 6144
ODD_IN = 2 * C_WIDTH + D_WIDTH

T_CTX = BATCH * SEQ
T_LAT = DEC_BATCH * DEC_SEQ
T_ALL = T_CTX + T_LAT
N_COND = 1 + DEC_BATCH
COND_PAD = 8

LANES = 128
VMEM_LIMIT = 56 * 1024 * 1024


def _cparams(sem):
    return pltpu.CompilerParams(dimension_semantics=sem, vmem_limit_bytes=VMEM_LIMIT)


def _mod_row(i, tm):
    start = i * tm
    return jnp.where(start < T_CTX, 0, 1 + (start - T_CTX) // DEC_SEQ)


def _ada_kernel(cond_ref, w_ref, b_ref, o_ref):
    c = cond_ref[...]
    s = (c * jax.nn.sigmoid(c)).astype(BF16)
    o_ref[...] = jnp.dot(s, w_ref[...].astype(BF16), preferred_element_type=F32) + b_ref[...]


def _ada(cond, w_ada, b_ada, tn=1024):
    depth, d, n = w_ada.shape
    return pl.pallas_call(
        _ada_kernel,
        grid=(depth, n // tn),
        in_specs=[pl.BlockSpec((COND_PAD, d), lambda l, j: (0, 0)),
                  pl.BlockSpec((None, d, tn), lambda l, j: (l, 0, j)),
                  pl.BlockSpec((None, 1, tn), lambda l, j: (l, 0, j))],
        out_specs=pl.BlockSpec((None, COND_PAD, tn), lambda l, j: (l, 0, j)),
        out_shape=jax.ShapeDtypeStruct((depth, COND_PAD, n), F32),
        compiler_params=_cparams(("parallel", "parallel")),
        name="ada",
    )(cond, w_ada, b_ada.reshape(depth, 1, n))


def _inproj_kernel(x_ref, mod_ref, w_ref, o_ref, h_ref):
    @pl.when(pl.program_id(1) == 0)
    def _():
        sh = mod_ref[0:1, :]
        sc = mod_ref[1:2, :]
        h_ref[...] = (x_ref[...] * (1.0 + sc) + sh).astype(BF16)

    o_ref[...] = jnp.dot(h_ref[...], w_ref[...], preferred_element_type=F32)


def _inproj(x, mods_l, w, tm=1024, tn=1024):
    t, d = x.shape
    n = w.shape[1]
    return pl.pallas_call(
        _inproj_kernel,
        grid=(t // tm, n // tn),
        in_specs=[pl.BlockSpec((tm, d), lambda i, j: (i, 0)),
                  pl.BlockSpec((None, 6, d), lambda i, j: (_mod_row(i, tm), 0, 0)),
                  pl.BlockSpec((d, tn), lambda i, j: (0, j))],
        out_specs=pl.BlockSpec((tm, tn), lambda i, j: (i, j)),
        out_shape=jax.ShapeDtypeStruct((t, n), F32),
        scratch_shapes=[pltpu.VMEM((tm, d), BF16)],
        compiler_params=_cparams(("parallel", "arbitrary")),
        name="inproj",
    )(x, mods_l, w)


def _layer_norm_rows(z, g, b):
    mu = jnp.mean(z, axis=-1, keepdims=True)
    zc = z - mu
    var = jnp.mean(zc * zc, axis=-1, keepdims=True)
    return zc * lax.rsqrt(var + LN_EPS) * g + b


def _outproj_kernel(ma_ref, mb_ref, x_ref, mod_ref, wa_ref, wb_ref, g_ref, b_ref, x1_ref, h2t_ref):
    o = jnp.dot(ma_ref[...], wa_ref[...], preferred_element_type=F32)
    o = o + jnp.dot(mb_ref[...], wb_ref[...], preferred_element_type=F32)
    g1 = mod_ref[2:3, :]
    x1 = _layer_norm_rows(DEEPNORM_ALPHA * x_ref[...] + g1 * o, g_ref[...], b_ref[...])
    x1_ref[...] = x1
    h2 = x1 * (1.0 + mod_ref[4:5, :]) + mod_ref[3:4, :]
    h2t_ref[...] = h2.T.astype(BF16)


def _outproj(mix_a, mix_b, x, mods_l, w_out, ln_g, ln_b, tm=512):
    t, d = x.shape
    ka = mix_a.shape[1]
    kb = mix_b.shape[1]
    return pl.pallas_call(
        _outproj_kernel,
        grid=(t // tm,),
        in_specs=[pl.BlockSpec((tm, ka), lambda i: (i, 0)),
                  pl.BlockSpec((tm, kb), lambda i: (i, 0)),
                  pl.BlockSpec((tm, d), lambda i: (i, 0)),
                  pl.BlockSpec((None, 6, d), lambda i: (_mod_row(i, tm), 0, 0)),
                  pl.BlockSpec((ka, d), lambda i: (0, 0)),
                  pl.BlockSpec((kb, d), lambda i: (1, 0)),
                  pl.BlockSpec((1, d), lambda i: (0, 0)),
                  pl.BlockSpec((1, d), lambda i: (0, 0))],
        out_specs=[pl.BlockSpec((tm, d), lambda i: (i, 0)),
                   pl.BlockSpec((d, tm), lambda i: (0, i))],
        out_shape=[jax.ShapeDtypeStruct((t, d), F32),
                   jax.ShapeDtypeStruct((d, t), BF16)],
        compiler_params=_cparams(("parallel",)),
        name="outproj_ln",
    )(mix_a, mix_b, x, mods_l, w_out, w_out, ln_g, ln_b)


def _split_bf16(x):
    hi = x.astype(BF16)
    lo = (x - hi.astype(F32)).astype(BF16)
    return hi, lo


def _wqk_kernel(k_ref, wq_ref, o_ref):
    kh, kl = _split_bf16(k_ref[...])
    wh, wl = _split_bf16(wq_ref[...])
    nt = (((1,), (1,)), ((), ()))
    acc = lax.dot_general(kh, wh, nt, preferred_element_type=F32)
    acc = acc + lax.dot_general(kh, wl, nt, preferred_element_type=F32)
    acc = acc + lax.dot_general(kl, wh, nt, preferred_element_type=F32)
    o_ref[...] = acc.astype(BF16)


def _wqk_fold(keys_l, wq_l):
    h, p, nk, half = keys_l.shape
    d = wq_l.shape[0]
    return pl.pallas_call(
        _wqk_kernel,
        grid=(h * p,),
        in_specs=[pl.BlockSpec((None, nk, half), lambda i: (i, 0, 0)),
                  pl.BlockSpec((d, half), lambda i: (0, i))],
        out_specs=pl.BlockSpec((nk, d), lambda i: (i, 0)),
        out_shape=jax.ShapeDtypeStruct((h * p * nk, d), BF16),
        compiler_params=_cparams(("parallel",)),
        name="peer_wqk_fold",
    )(keys_l.reshape(h * p, nk, half), wq_l)


def _scores_kernel(w_ref, h_ref, o_ref):
    o_ref[...] = jnp.dot(w_ref[...], h_ref[...], preferred_element_type=F32)


def _scores(wqk_t, h2t, tn=512):
    n, d = wqk_t.shape
    t = h2t.shape[1]
    return pl.pallas_call(
        _scores_kernel,
        grid=(t // tn,),
        in_specs=[pl.BlockSpec((n, d), lambda i: (0, 0)),
                  pl.BlockSpec((d, tn), lambda i: (0, i))],
        out_specs=pl.BlockSpec((n, tn), lambda i: (0, i)),
        out_shape=jax.ShapeDtypeStruct((n, t), F32),
        compiler_params=_cparams(("parallel",)),
        name="peer_scores",
    )(wqk_t, h2t)


def _gelu_tanh(x):
    c = math.sqrt(2.0 / math.pi)
    return 0.5 * x * (1.0 + jnp.tanh(c * (x + 0.044715 * (x * x * x))))


def _peer_kernel(h2t_ref, u_ref, vt_ref, r2_ref, e2_ref, lim_ref, w1_ref, x1_ref, mod_ref, g_ref, b_ref,
                 o_ref, acc_ref, w_ref, *, n_i, sub_i, n_heads, nk):
    eb = pl.program_id(1)

    @pl.when(eb == 0)
    def _():
        acc_ref[...] = jnp.zeros_like(acc_ref)

    for sb in range(n_i // sub_i):
        rows = slice(sb * sub_i * nk, (sb + 1) * sub_i * nk)
        a = jnp.dot(u_ref[rows, :], h2t_ref[...], preferred_element_type=F32)
        for si in range(sub_i):
            ii = sb * sub_i + si
            gate = None
            for h in range(n_heads):
                lim = lim_ref[h, ii:ii + 1, :].astype(BF16)
                w1 = w1_ref[h, ii:ii + 1, :].astype(BF16)
                term = jnp.where(r2_ref[h] < lim, e2_ref[h], jnp.zeros((), BF16)) * w1
                gate = term if gate is None else gate + term
            act = _gelu_tanh(a[si * nk:(si + 1) * nk, :]).astype(BF16)
            w_ref[ii * nk:(ii + 1) * nk, :] = act * gate
    acc_ref[...] += jnp.dot(vt_ref[...], w_ref[...], preferred_element_type=F32)

    @pl.when(eb == pl.num_programs(1) - 1)
    def _():
        g2 = mod_ref[5:6, :]
        tm = acc_ref.shape[1]
        for c in range(tm // LANES):
            rows = slice(c * LANES, (c + 1) * LANES)
            y = acc_ref[:, rows].T
            z = DEEPNORM_ALPHA * x1_ref[rows, :] + g2 * y
            o_ref[rows, :] = _layer_norm_rows(z, g_ref[...], b_ref[...])


def _peer_dense(h2t, u, vt, r2t, e2t, lim1t, w1t, x1, mods_l, ln_g, ln_b, tm=512, n_i=8, sub_i=4):
    d, t = h2t.shape
    nexp = u.shape[0]
    n_heads, nk, _ = r2t.shape
    te = n_i * nk
    kern = functools.partial(_peer_kernel, n_i=n_i, sub_i=sub_i, n_heads=n_heads, nk=nk)
    return pl.pallas_call(
        kern,
        grid=(t // tm, nexp // te),
        in_specs=[pl.BlockSpec((d, tm), lambda i, e: (0, i)),
                  pl.BlockSpec((te, d), lambda i, e: (e, 0)),
                  pl.BlockSpec((d, te), lambda i, e: (0, e)),
                  pl.BlockSpec((n_heads, nk, tm), lambda i, e: (0, 0, i)),
                  pl.BlockSpec((n_heads, nk, tm), lambda i, e: (0, 0, i)),
                  pl.BlockSpec((n_heads, n_i, tm), lambda i, e: (0, e, i)),
                  pl.BlockSpec((n_heads, n_i, tm), lambda i, e: (0, e, i)),
                  pl.BlockSpec((tm, d), lambda i, e: (i, 0)),
                  pl.BlockSpec((None, 6, d), lambda i, e: (_mod_row(i, tm), 0, 0)),
                  pl.BlockSpec((1, d), lambda i, e: (0, 0)),
                  pl.BlockSpec((1, d), lambda i, e: (0, 0))],
        out_specs=pl.BlockSpec((tm, d), lambda i, e: (i, 0)),
        out_shape=jax.ShapeDtypeStruct((t, d), F32),
        scratch_shapes=[pltpu.VMEM((d, tm), F32), pltpu.VMEM((te, tm), BF16)],
        compiler_params=_cparams(("parallel", "arbitrary")),
        name="peer_dense",
    )(h2t, u, vt, r2t, e2t, lim1t, w1t, x1, mods_l, ln_g, ln_b)


def _topk_rows(s, k, stable):
    n, lanes = s.shape
    row = lax.broadcasted_iota(jnp.int32, s.shape, 0).astype(F32)
    row_k = lax.broadcasted_iota(jnp.int32, (k, lanes), 0)
    rank = jnp.full(s.shape, float(k), F32)
    vals = jnp.zeros((k, lanes), F32)
    for r in range(k):
        m = jnp.max(s, axis=0, keepdims=True)
        sel = s == m
        if stable:
            first = jnp.min(jnp.where(sel, row, float(n)), axis=0, keepdims=True)
            sel = row == first
        rank = jnp.where(sel, float(r), rank)
        vals = jnp.where(row_k == r, m, vals)
        s = jnp.where(sel, -jnp.inf, s)
    return rank, vals


_CAND_ROWS = 16 + 7 * 8 + 8


def _route_tile(s1, s2, stable):
    k = PEER_TOPK
    rank1, v1 = _topk_rows(s1, k, stable)
    rank2, v2 = _topk_rows(s2, k, stable)

    groups = [v1[0:1, :] + v2] + [v1[a:a + 1, :] + v2[0:8, :] for a in range(1, 8)] + [v1[8:16, :] + v2[0:1, :]]
    cand = jnp.concatenate(groups, axis=0)
    r = lax.broadcasted_iota(jnp.int32, cand.shape, 0)
    flat = jnp.where(r < 16, r, jnp.where(r < 72, (1 + (r - 16) // 8) * k + (r - 16) % 8, (r - 64) * k)).astype(F32)
    top = v1[0:1, :] + v2[0:1, :]
    picked = jnp.zeros(cand.shape, F32)
    z = jnp.zeros((1, cand.shape[1]), F32)
    for _ in range(k):
        m = jnp.max(cand, axis=0, keepdims=True)
        sel = cand == m
        if stable:
            first = jnp.min(jnp.where(sel, flat, float(k * k)), axis=0, keepdims=True)
            sel = flat == first
        picked = jnp.where(sel, 1.0, picked)
        z = z + jnp.exp(m - top)
        cand = jnp.where(sel, -jnp.inf, cand)

    n_rows = [jnp.sum(picked[0:16, :], axis=0, keepdims=True)]
    n_rows += [jnp.sum(picked[16 + 8 * (a - 1):16 + 8 * a, :], axis=0, keepdims=True) for a in range(1, 8)]
    n_rows += [picked[64 + a:65 + a, :] for a in range(8, 16)]
    lim = jnp.zeros(s1.shape, F32)
    for a in range(k):
        lim = jnp.where(rank1 == float(a), n_rows[a], lim)

    def n_selected(rank):
        return jnp.sum(jnp.where(rank < float(k), 1.0, 0.0), axis=0, keepdims=True)

    tie = ((n_selected(rank1) != float(k)) | (n_selected(rank2) != float(k))
           | (jnp.sum(picked, axis=0, keepdims=True) != float(k)))
    e2 = jnp.exp(s2 - v2[0:1, :])
    w1 = jnp.exp(s1 - v1[0:1, :]) * (1.0 / z)
    return rank2, e2, lim, w1, tie


def _routing_kernel(s_ref, r2_ref, e2_ref, lim_ref, w1_ref, *, n_heads, nk, n_sub, sw):
    def body(it, carry):
        h = it // n_sub
        lanes = pl.ds(pl.multiple_of((it % n_sub) * sw, sw), sw)
        rows1 = pl.ds(pl.multiple_of(h * 2 * nk, nk), nk)
        rows2 = pl.ds(pl.multiple_of(h * 2 * nk + nk, nk), nk)

        def route(stable):
            rank2, e2, lim, w1, tie = _route_tile(s_ref[rows1, lanes], s_ref[rows2, lanes], stable)
            r2_ref[h, :, lanes] = rank2.astype(BF16)
            e2_ref[h, :, lanes] = e2.astype(BF16)
            lim_ref[h, :, lanes] = lim
            w1_ref[h, :, lanes] = w1
            return tie

        tie = route(stable=False)

        @pl.when(jnp.max(jnp.where(tie, 1.0, 0.0)) > 0.0)
        def _():
            route(stable=True)

        return carry

    lax.fori_loop(0, n_heads * n_sub, body, 0)


def _peer_routing(scores_t, tl=512, sw=256):
    assert PEER_TOPK == 16 and _CAND_ROWS == 80
    n, t = scores_t.shape
    nh, nk = PEER_HEADS, PEER_NKEYS
    kern = functools.partial(_routing_kernel, n_heads=nh, nk=nk, n_sub=tl // sw, sw=sw)
    spec = pl.BlockSpec((nh, nk, tl), lambda i: (0, 0, i))
    return pl.pallas_call(
        kern,
        grid=(t // tl,),
        in_specs=[pl.BlockSpec((n, tl), lambda i: (0, i))],
        out_specs=[spec, spec, spec, spec],
        out_shape=[jax.ShapeDtypeStruct((nh, nk, t), BF16), jax.ShapeDtypeStruct((nh, nk, t), BF16),
                   jax.ShapeDtypeStruct((nh, nk, t), F32), jax.ShapeDtypeStruct((nh, nk, t), F32)],
        compiler_params=_cparams(("parallel",)),
        name="peer_routing",
    )(scores_t)


def _rope_tables(n_tokens, dim):
    rows = n_tokens // GRID_W
    row = jnp.repeat(jnp.arange(rows, dtype=F32), GRID_W)
    col = jnp.tile(jnp.arange(GRID_W, dtype=F32), rows)
    quarter = dim // 4
    freqs = ROPE_BASE ** (-jnp.arange(quarter, dtype=F32) / quarter)
    ang = jnp.concatenate([row[:, None] * freqs, col[:, None] * freqs], axis=-1)
    cos, sin = jnp.cos(ang), jnp.sin(ang)
    reps = LANES // dim
    cos_full = jnp.tile(jnp.concatenate([cos, cos], axis=-1), (1, reps))
    sin_full = jnp.tile(jnp.concatenate([-sin, sin], axis=-1), (1, reps))
    return cos_full, sin_full


def _rope_kernel(x_ref, cos_ref, sin_ref, o_ref, *, half, scale):
    x = x_ref[...]
    w = x.shape[1]
    lane = lax.broadcasted_iota(jnp.int32, x.shape, 1)
    partner = jnp.where((lane % (2 * half)) < half, pltpu.roll(x, w - half, 1), pltpu.roll(x, half, 1))
    o_ref[...] = ((x * cos_ref[...] + partner * sin_ref[...]) * scale).astype(o_ref.dtype)


def _rope(proj, col0, ncols, row0, nrows, cos_full, sin_full, half, scale=1.0, tm=512):
    kern = functools.partial(_rope_kernel, half=half, scale=scale)
    cb0 = col0 // LANES
    rb0 = row0 // tm
    return pl.pallas_call(
        kern,
        grid=(nrows // tm, ncols // LANES),
        in_specs=[pl.BlockSpec((tm, LANES), lambda i, j: (rb0 + i, cb0 + j)),
                  pl.BlockSpec((tm, LANES), lambda i, j: (i % (cos_full.shape[0] // tm), 0)),
                  pl.BlockSpec((tm, LANES), lambda i, j: (i % (cos_full.shape[0] // tm), 0))],
        out_specs=pl.BlockSpec((tm, LANES), lambda i, j: (i, j)),
        out_shape=jax.ShapeDtypeStruct((nrows, ncols), BF16),
        compiler_params=_cparams(("parallel", "parallel")),
        name="rope",
    )(proj, cos_full, sin_full)


def _attn_kernel(*refs, tq, l_self, l_ctx, lam_init):
    if l_ctx:
        lam_ref, q_ref, k_ref, v_ref, ck_ref, cv_ref, g_ref, o_ref, ks_ref, vs_ref = refs
    else:
        lam_ref, q_ref, k_ref, v_ref, g_ref, o_ref, ks_ref, vs_ref = refs

    @pl.when(pl.program_id(2) == 0)
    def _():
        if l_ctx:
            ks_ref[0:l_ctx, :] = ck_ref[...].astype(BF16)
            vs_ref[0:l_ctx, :] = cv_ref[...].astype(BF16)
        ks_ref[l_ctx:l_ctx + l_self, :] = k_ref[...].astype(BF16)
        vs_ref[l_ctx:l_ctx + l_self, :] = v_ref[...].astype(BF16)

    lv = lam_ref[...]
    lam = (jnp.exp(jnp.sum(lv[0:1, :] * lv[1:2, :], axis=-1, keepdims=True))
           - jnp.exp(jnp.sum(lv[2:3, :] * lv[3:4, :], axis=-1, keepdims=True)) + lam_init)
    q = q_ref[...].astype(F32) * (A_QK_DIM ** -0.5)
    lane = lax.broadcasted_iota(jnp.int32, q.shape, 1)
    first = lane < A_QK_DIM
    q0 = jnp.where(first, q, 0.0).astype(BF16)
    q1 = jnp.where(first, 0.0, q).astype(BF16)
    qs = jnp.concatenate([q0, q1], axis=0)
    s = lax.dot_general(qs, ks_ref[...], (((1,), (1,)), ((), ())), preferred_element_type=F32)
    m = jnp.max(s, axis=-1, keepdims=True)
    p = jnp.exp(s - m)
    p = p / jnp.sum(p, axis=-1, keepdims=True)
    w = (p[0:tq, :] - lam * p[tq:2 * tq, :]).astype(BF16)
    o = jnp.dot(w, vs_ref[...], preferred_element_type=F32)
    o = o * lax.rsqrt(jnp.mean(o * o, axis=-1, keepdims=True) + LN_EPS) * g_ref[...] * (1.0 - lam_init)
    o_ref[...] = o.astype(o_ref.dtype)


def _diff_attention(q_src, k_src, v_src, nb, l_self, lam_vec, norm_g, lam_init, ctx_k=None, ctx_v=None, tq=256):
    (q_arr, q_col0, q_row0), (k_arr, k_col0, k_row0), (v_arr, v_col0, v_row0) = q_src, k_src, v_src
    l_ctx = 0 if ctx_k is None else ctx_k.shape[2]
    nq = l_self // tq
    qb0, kb0, vb0 = q_col0 // LANES, k_col0 // LANES, v_col0 // LANES
    rq0, rk0, rv0 = q_row0 // tq, k_row0 // l_self, v_row0 // l_self
    kern = functools.partial(_attn_kernel, tq=tq, l_self=l_self, l_ctx=l_ctx, lam_init=lam_init)
    in_specs = [pl.BlockSpec((4, A_QK_DIM), lambda b, h, i: (0, 0)),
                pl.BlockSpec((tq, LANES), lambda b, h, i: (rq0 + b * nq + i, qb0 + h)),
                pl.BlockSpec((l_self, LANES), lambda b, h, i: (rk0 + b, kb0 + h)),
                pl.BlockSpec((l_self, LANES), lambda b, h, i: (rv0 + b, vb0 + h))]
    args = [lam_vec, q_arr, k_arr, v_arr]
    if l_ctx:
        in_specs += [pl.BlockSpec((None, None, l_ctx, LANES), lambda b, h, i: (b, h, 0, 0)),
                     pl.BlockSpec((None, None, l_ctx, LANES), lambda b, h, i: (b, h, 0, 0))]
        args += [ctx_k, ctx_v]
    in_specs += [pl.BlockSpec((None, 1, LANES), lambda b, h, i: (h, 0, 0))]
    args += [norm_g.reshape(A_HEADS, 1, A_V_DIM)]
    return pl.pallas_call(
        kern,
        grid=(nb, A_HEADS, nq),
        in_specs=in_specs,
        out_specs=pl.BlockSpec((tq, LANES), lambda b, h, i: (b * nq + i, h)),
        out_shape=jax.ShapeDtypeStruct((nb * l_self, A_HEADS * A_V_DIM), BF16),
        scratch_shapes=[pltpu.VMEM((l_ctx + l_self, LANES), BF16),
                        pltpu.VMEM((l_ctx + l_self, LANES), BF16)],
        compiler_params=_cparams(("parallel", "parallel", "arbitrary")),
        name="diff_attention",
    )(*args)


def _ret_kernel(*refs, l_self, has_state, k_scale):
    if has_state:
        dec_ref, q_ref, k_ref, v_ref, g_ref, ng_ref, s0_ref, o_ref, st_ref, of_ref = refs
    else:
        dec_ref, q_ref, k_ref, v_ref, g_ref, ng_ref, o_ref, st_ref, of_ref = refs
    c = RET_CHUNK
    nc = l_self // c
    lg = jax.nn.log_sigmoid(dec_ref[...])
    pos_i = lax.broadcasted_iota(jnp.int32, (c, c), 0).astype(F32)
    pos_j = lax.broadcasted_iota(jnp.int32, (c, c), 1).astype(F32)
    col = lax.broadcasted_iota(jnp.int32, (c, 1), 0).astype(F32)
    nt = (((1,), (1,)), ((), ()))

    def direction(d):
        lg_row = lg[d:d + 1, :]
        lg1 = lg_row[:, 0:1]
        dist = (pos_i - pos_j) if d == 0 else (pos_j - pos_i)
        intra = jnp.where(dist >= 0, jnp.exp(lg_row * jnp.maximum(dist, 0.0)), 0.0)
        q_dec = jnp.exp(lg1 * (col + 1.0)) if d == 0 else jnp.exp(lg1 * (c - col))
        k_dec = jnp.exp(lg1 * (c - 1.0 - col)) if d == 0 else jnp.exp(lg1 * col)
        chunk_dec = jnp.exp(lg1 * float(c))
        s_init = s0_ref[d].astype(F32) if has_state else jnp.zeros((B_QK_DIM, B_V_DIM), F32)

        def step(it, s):
            ci = it if d == 0 else nc - 1 - it
            rows = pl.ds(pl.multiple_of(ci * c, c), c)
            qc = q_ref[rows, :].astype(BF16)
            kf = k_ref[rows, :].astype(F32) * k_scale
            vc = v_ref[rows, :].astype(BF16)
            sc = lax.dot_general(qc, kf.astype(BF16), nt, preferred_element_type=F32) * intra
            o = jnp.dot(sc.astype(BF16), vc, preferred_element_type=F32)
            o = o + jnp.dot(qc, s.astype(BF16), preferred_element_type=F32) * q_dec
            if d == 0:
                of_ref[rows, :] = o
            else:
                o = o + of_ref[rows, :]
                mu = jnp.mean(o, axis=-1, keepdims=True)
                oc = o - mu
                var = jnp.mean(oc * oc, axis=-1, keepdims=True)
                gate = g_ref[rows, :]
                gate = gate * jax.nn.sigmoid(gate)
                o_ref[rows, :] = (oc * lax.rsqrt(var + LN_EPS) * ng_ref[...] * gate).astype(o_ref.dtype)
            kd = (kf * k_dec).T.astype(BF16)
            return s * chunk_dec + jnp.dot(kd, vc, preferred_element_type=F32)

        st_ref[d] = lax.fori_loop(0, nc, step, s_init)

    direction(0)
    direction(1)


def _retention(q_src, k_src, v_src, g_src, nb, l_self, decay_logit, norm_g, k_scale, s0=None):
    (q_arr, q_col0, q_row0), (k_arr, k_col0, k_row0) = q_src, k_src
    (v_arr, v_col0, v_row0), (g_arr, g_col0, g_row0) = v_src, g_src
    has_state = s0 is not None
    kern = functools.partial(_ret_kernel, l_self=l_self, has_state=has_state, k_scale=k_scale)
    dec = jnp.broadcast_to(decay_logit.T[:, :, None], (B_HEADS, 2, LANES)).astype(F32)
    qc0, kc0 = q_col0 // B_QK_DIM, k_col0 // B_QK_DIM
    vc0, gc0 = v_col0 // B_V_DIM, g_col0 // B_V_DIM
    rq, rk, rv, rg = q_row0 // l_self, k_row0 // l_self, v_row0 // l_self, g_row0 // l_self
    in_specs = [pl.BlockSpec((None, 2, LANES), lambda b, h: (h, 0, 0)),
                pl.BlockSpec((l_self, B_QK_DIM), lambda b, h: (rq + b, qc0 + h)),
                pl.BlockSpec((l_self, B_QK_DIM), lambda b, h: (rk + b, kc0 + h)),
                pl.BlockSpec((l_self, B_V_DIM), lambda b, h: (rv + b, vc0 + h)),
                pl.BlockSpec((l_self, B_V_DIM), lambda b, h: (rg + b, gc0 + h)),
                pl.BlockSpec((None, 1, B_V_DIM), lambda b, h: (h, 0, 0))]
    args = [dec, q_arr, k_arr, v_arr, g_arr, norm_g.reshape(B_HEADS, 1, B_V_DIM)]
    state_spec = pl.BlockSpec((None, 2, None, B_QK_DIM, B_V_DIM), lambda b, h: (b, 0, h, 0, 0))
    if has_state:
        in_specs.append(state_spec)
        args.append(s0)
    return pl.pallas_call(
        kern,
        grid=(nb, B_HEADS),
        in_specs=in_specs,
        out_specs=[pl.BlockSpec((l_self, B_V_DIM), lambda b, h: (b, h)), state_spec],
        out_shape=[jax.ShapeDtypeStruct((nb * l_self, B_HEADS * B_V_DIM), BF16),
                   jax.ShapeDtypeStruct((nb, 2, B_HEADS, B_QK_DIM, B_V_DIM), F32)],
        scratch_shapes=[pltpu.VMEM((l_self, B_V_DIM), F32)],
        compiler_params=_cparams(("parallel", "parallel")),
        name="retention",
    )(*args)


def _sgu_kernel(u_ref, v_ref, g_ref, b_ref, ws_ref, bs_ref, o_ref):
    vn = _layer_norm_rows(v_ref[...], g_ref[...], b_ref[...]).astype(BF16)
    tm = vn.shape[0]
    gw = C_WIDTH // C_GROUPS
    for gi in range(C_GROUPS):
        w = ws_ref[gi].astype(BF16)
        bias = bs_ref[:, gi:gi + 1]
        cols = slice(gi * gw, (gi + 1) * gw)
        for ci in range(tm // C_CHUNK):
            rows = slice(ci * C_CHUNK, (ci + 1) * C_CHUNK)
            mixed = jnp.dot(w, vn[rows, cols], preferred_element_type=F32) + bias
            o_ref[rows, cols] = (u_ref[rows, cols] * mixed).astype(o_ref.dtype)


def _sgu(proj, sg_g, sg_b, w_s, b_s, tm=512):
    t = proj.shape[0]
    return pl.pallas_call(
        _sgu_kernel,
        grid=(t // tm,),
        in_specs=[pl.BlockSpec((tm, C_WIDTH), lambda i: (i, 0)),
                  pl.BlockSpec((tm, C_WIDTH), lambda i: (i, 1)),
                  pl.BlockSpec((1, C_WIDTH), lambda i: (0, 0)),
                  pl.BlockSpec((1, C_WIDTH), lambda i: (0, 0)),
                  pl.BlockSpec((C_GROUPS, C_CHUNK, C_CHUNK), lambda i: (0, 0, 0)),
                  pl.BlockSpec((C_CHUNK, C_GROUPS), lambda i: (0, 0))],
        out_specs=pl.BlockSpec((tm, C_WIDTH), lambda i: (i, 0)),
        out_shape=jax.ShapeDtypeStruct((t, C_WIDTH), BF16),
        compiler_params=_cparams(("parallel",)),
        name="sgu",
    )(proj, proj, sg_g[None, :], sg_b[None, :], w_s, b_s.T)


def _pool_kernel(p_ref, w_ref, sc_ref, o_ref, *, l_seq):
    p = p_ref[...]
    n = p.shape[0]
    pos = (lax.broadcasted_iota(jnp.int32, p.shape, 0) % l_seq)
    w_mat = w_ref[...].astype(BF16)
    for gi, win in enumerate(POOL_WINDOWS):
        @pl.when(pl.program_id(1) == gi)
        def _(win=win):
            half = win // 2
            acc = jnp.zeros(p.shape, F32)
            for d in range(-half, half):
                shifted = p if d == 0 else pltpu.roll(p, (-d) % n, 0)
                ok = (pos + d >= 0) & (pos + d <= l_seq - 1)
                acc = acc + jnp.where(ok, shifted, 0.0)
            cnt = jnp.minimum(pos + (half - 1), l_seq - 1) - jnp.maximum(pos - half, 0) + 1
            pooled = acc / cnt.astype(F32) - p
            mixed = jnp.dot(pooled.astype(BF16), w_mat, preferred_element_type=F32)
            o_ref[...] = (mixed * sc_ref[...]).astype(o_ref.dtype)


def _pool(proj, col0, row0, nrows, l_seq, p_w, p_scale, rows_per_step):
    kern = functools.partial(_pool_kernel, l_seq=l_seq)
    cb0 = col0 // D_GROUP
    rb0 = row0 // rows_per_step
    return pl.pallas_call(
        kern,
        grid=(nrows // rows_per_step, N_POOL),
        in_specs=[pl.BlockSpec((rows_per_step, D_GROUP), lambda i, g: (rb0 + i, cb0 + g)),
                  pl.BlockSpec((None, D_GROUP, D_GROUP), lambda i, g: (g, 0, 0)),
                  pl.BlockSpec((1, D_GROUP), lambda i, g: (0, g))],
        out_specs=pl.BlockSpec((rows_per_step, D_GROUP), lambda i, g: (i, g)),
        out_shape=jax.ShapeDtypeStruct((nrows, D_WIDTH), BF16),
        compiler_params=_cparams(("parallel", "parallel")),
        name="pool",
    )(proj, p_w, p_scale[None, :])


QA0, KA0, VA0 = 0, 1024, 2048
QB0, KB0, VB0, GB0 = 3072, 3584, 4096, 5120
B_QK_W = B_HEADS * B_QK_DIM


def _even_mixer(proj, e, l, diff_lambda, diff_norm_g, ret_decay_logit, ret_norm_g,
                cache_attn_k, cache_attn_v, state_retention, rope_a, rope_b):
    lam_init = 0.8 - 0.6 * math.exp(-0.3 * l)
    oa_ctx = _diff_attention((proj, QA0, 0), (proj, KA0, 0), (proj, VA0, 0), BATCH, SEQ,
                             diff_lambda, diff_norm_g, lam_init, tq=SEQ)
    qa_r = _rope(proj, QA0, MIX_HALF, T_CTX, T_LAT, *rope_a, half=A_QK_DIM // 2)
    ka_r = _rope(proj, KA0, MIX_HALF, T_CTX, T_LAT, *rope_a, half=A_QK_DIM // 2)
    oa_lat = _diff_attention((qa_r, 0, 0), (ka_r, 0, 0), (proj, VA0, T_CTX), DEC_BATCH, DEC_SEQ,
                             diff_lambda, diff_norm_g, lam_init,
                             ctx_k=cache_attn_k[:, e], ctx_v=cache_attn_v[:, e], tq=256)
    mix_a = jnp.concatenate([oa_ctx, oa_lat], axis=0)

    k_scale = B_QK_DIM ** -0.5
    qb_r = _rope(proj, QB0, B_QK_W, T_CTX, T_LAT, *rope_b, half=B_QK_DIM // 2)
    kb_r = _rope(proj, KB0, B_QK_W, T_CTX, T_LAT, *rope_b, half=B_QK_DIM // 2, scale=k_scale)

    ob_ctx, st_ctx = _retention((proj, QB0, 0), (proj, KB0, 0), (proj, VB0, 0), (proj, GB0, 0), BATCH, SEQ,
                                ret_decay_logit, ret_norm_g, k_scale)
    ob_lat, _ = _retention((qb_r, 0, 0), (kb_r, 0, 0), (proj, VB0, T_CTX), (proj, GB0, T_CTX), DEC_BATCH, DEC_SEQ,
                           ret_decay_logit, ret_norm_g, 1.0, s0=state_retention[:, e])
    mix_b = jnp.concatenate([ob_ctx, ob_lat], axis=0)

    pc = proj[:T_CTX]
    ka = pc[:, KA0:VA0].reshape(BATCH, SEQ, A_HEADS, 2 * A_QK_DIM).transpose(0, 2, 1, 3)
    va = pc[:, VA0:QB0].reshape(BATCH, SEQ, A_HEADS, A_V_DIM).transpose(0, 2, 1, 3)
    return mix_a, mix_b, ka, va, st_ctx


def _odd_mixer(proj, sg_g, sg_b, w_s, b_s, p_w, p_scale):
    mix_c = _sgu(proj, sg_g, sg_b, w_s, b_s)
    pool_ctx = _pool(proj, 2 * C_WIDTH, 0, T_CTX, SEQ, p_w, p_scale, rows_per_step=8 * SEQ)
    pool_lat = _pool(proj, 2 * C_WIDTH, T_CTX, T_LAT, DEC_SEQ, p_w, p_scale, rows_per_step=DEC_SEQ)
    return mix_c, jnp.concatenate([pool_ctx, pool_lat], axis=0)


def kernel(x_prompt, x_sample, cache_attn_k, cache_attn_v, state_retention, c, c_ctx, w_ada, b_ada, ln_g, ln_b,
           w_in_even, diff_lambda, diff_norm_g, ret_decay_logit, ret_norm_g, w_in_odd, sgu_ln_g, sgu_ln_b,
           sgu_w, sgu_b, pool_w, pool_scale, w_out, peer_wq, peer_keys, peer_u, peer_v):
    x = jnp.concatenate([x_prompt.reshape(T_CTX, D_MODEL), x_sample.reshape(T_LAT, D_MODEL)], axis=0)
    cond = jnp.concatenate([c_ctx[None, :], c, jnp.zeros((COND_PAD - N_COND, D_MODEL), F32)], axis=0)
    mods = _ada(cond, w_ada, b_ada)[:, :N_COND].reshape(DEPTH, N_COND, 6, D_MODEL)

    rope_a = _rope_tables(DEC_SEQ, A_QK_DIM)
    rope_b = _rope_tables(DEC_SEQ, B_QK_DIM)
    w_in_even_b = w_in_even.astype(BF16)
    w_in_odd_b = w_in_odd.astype(BF16)
    w_out_b = w_out.astype(BF16)
    peer_u_b = peer_u.astype(BF16)
    peer_vt_b = peer_v.astype(BF16).transpose(0, 2, 1)

    ks, vs, ss = [], [], []
    for l in range(DEPTH):
        mods_l = mods[l]
        if l % 2 == 0:
            e = l // 2
            proj = _inproj(x, mods_l, w_in_even_b[e])
            mix_a, mix_b, ka, va, st = _even_mixer(
                proj, e, l, diff_lambda[e], diff_norm_g[e], ret_decay_logit[e], ret_norm_g[e],
                cache_attn_k, cache_attn_v, state_retention, rope_a, rope_b)
            ks.append(ka)
            vs.append(va)
            ss.append(st)
        else:
            o = l // 2
            proj = _inproj(x, mods_l, w_in_odd_b[o])
            mix_a, mix_b = _odd_mixer(proj, sgu_ln_g[o], sgu_ln_b[o], sgu_w[o], sgu_b[o], pool_w[o], pool_scale[o])
        x1, h2t = _outproj(mix_a, mix_b, x, mods_l, w_out_b[l], ln_g[l, 0][None, :], ln_b[l, 0][None, :])
        wqk_t = _wqk_fold(peer_keys[l], peer_wq[l])
        scores_t = _scores(wqk_t, h2t)
        r2t, e2t, lim1t, w1t = _peer_routing(scores_t)
        x = _peer_dense(h2t, peer_u_b[l], peer_vt_b[l], r2t, e2t, lim1t, w1t, x1, mods_l,
                        ln_g[l, 1][None, :], ln_b[l, 1][None, :])

    y_prompt = x[:T_CTX].reshape(BATCH, SEQ, D_MODEL)
    y_sample = x[T_CTX:].reshape(DEC_BATCH, DEC_SEQ, D_MODEL)
    return (y_prompt, y_sample, jnp.stack(ks, axis=1), jnp.stack(vs, axis=1), jnp.stack(ss, axis=1))
```

```python
import functools
import math

import jax
import jax.numpy as jnp
from jax import lax
from jax.experimental import pallas as pl
from jax.experimental.pallas import tpu as pltpu

F32 = jnp.float32
BF16 = jnp.bfloat16

D_MODEL = 2048
BATCH = 32
SEQ = 256
DEPTH = 4
DEC_BATCH = 2
DEC_SEQ = 4096
PAST_LEN = 256
GRID_W = 64
N_EVEN = (DEPTH + 1) // 2
MIX_HALF = D_MODEL // 2
A_HEADS = 8
A_QK_DIM = 64
A_V_DIM = MIX_HALF // A_HEADS
B_HEADS = 4
B_QK_DIM = 128
B_V_DIM = MIX_HALF // B_HEADS
RET_CHUNK = 128
C_WIDTH = MIX_HALF
C_GROUPS = 4
C_CHUNK = 128
D_WIDTH = MIX_HALF
POOL_WINDOWS = (2, 4, 8, 16)
N_POOL = 4
D_GROUP = D_WIDTH // N_POOL
PEER_HEADS = 8
PEER_NKEYS = 128
PEER_NEXP = PEER_NKEYS * PEER_NKEYS
PEER_QDIM = 256
PEER_TOPK = 16
ROPE_BASE = 10000.0
LN_EPS = 1e-5
DEEPNORM_ALPHA = (2 * DEPTH) ** 0.25
EVEN_IN = 6144
ODD_IN = 2 * C_WIDTH + D_WIDTH

T_CTX = BATCH * SEQ
T_LAT = DEC_BATCH * DEC_SEQ
T_ALL = T_CTX + T_LAT
N_COND = 1 + DEC_BATCH
COND_PAD = 8

LANES = 128
VMEM_LIMIT = 56 * 1024 * 1024


def _cparams(sem):
    return pltpu.CompilerParams(dimension_semantics=sem, vmem_limit_bytes=VMEM_LIMIT)


def _mod_row(i, tm):
    start = i * tm
    return jnp.where(start < T_CTX, 0, 1 + (start - T_CTX) // DEC_SEQ)


def _ada_kernel(cond_ref, w_ref, b_ref, o_ref):
    c = cond_ref[...]
    s = (c * jax.nn.sigmoid(c)).astype(BF16)
    o_ref[...] = jnp.dot(s, w_ref[...].astype(BF16), preferred_element_type=F32) + b_ref[...]


def _ada(cond, w_ada, b_ada, tn=1024):
    depth, d, n = w_ada.shape
    return pl.pallas_call(
        _ada_kernel,
        grid=(depth, n // tn),
        in_specs=[pl.BlockSpec((COND_PAD, d), lambda l, j: (0, 0)),
                  pl.BlockSpec((None, d, tn), lambda l, j: (l, 0, j)),
                  pl.BlockSpec((None, 1, tn), lambda l, j: (l, 0, j))],
        out_specs=pl.BlockSpec((None, COND_PAD, tn), lambda l, j: (l, 0, j)),
        out_shape=jax.ShapeDtypeStruct((depth, COND_PAD, n), F32),
        compiler_params=_cparams(("parallel", "parallel")),
        name="ada",
    )(cond, w_ada, b_ada.reshape(depth, 1, n))


def _modulate_into(h_ref, x_ref, mod_ref):
    @pl.when(pl.program_id(1) == 0)
    def _():
        sh = mod_ref[0:1, :]
        sc = mod_ref[1:2, :]
        h_ref[...] = (x_ref[...] * (1.0 + sc) + sh).astype(BF16)


def _inproj_kernel(x_ref, mod_ref, w_ref, o_ref, h_ref):
    _modulate_into(h_ref, x_ref, mod_ref)
    o_ref[...] = jnp.dot(h_ref[...], w_ref[...], preferred_element_type=F32)


def _rotate_lane_group(x, cos, sin, half):
    if 2 * half == LANES:
        partner = pltpu.roll(x, half, 1)
    else:
        lane = lax.broadcasted_iota(jnp.int32, x.shape, 1)
        partner = jnp.where((lane % (2 * half)) < half, pltpu.roll(x, LANES - half, 1), pltpu.roll(x, half, 1))
    return x * cos + partner * sin


def _inproj_rope_kernel(x_ref, mod_ref, w_ref, ca_ref, sa_ref, cb_ref, sb_ref, o_ref, h_ref, *, first_lat_tile, plan):
    _modulate_into(h_ref, x_ref, mod_ref)
    o_ref[...] = jnp.dot(h_ref[...], w_ref[...], preferred_element_type=F32)
    latent = pl.program_id(0) >= first_lat_tile
    for j, groups in plan.items():
        @pl.when(latent & (pl.program_id(1) == j))
        def _(groups=groups):
            for g, kind, scale in groups:
                cols = slice(g * LANES, (g + 1) * LANES)
                cos, sin = (ca_ref, sa_ref) if kind == 'a' else (cb_ref, sb_ref)
                half = (A_QK_DIM if kind == 'a' else B_QK_DIM) // 2
                y = _rotate_lane_group(o_ref[:, cols], cos[...], sin[...], half)
                o_ref[:, cols] = y if scale == 1.0 else y * scale


def _inproj(x, mods_l, w, rope=None, tm=1024, tn=1024):
    t, d = x.shape
    n = w.shape[1]
    in_specs = [pl.BlockSpec((tm, d), lambda i, j: (i, 0)),
                pl.BlockSpec((None, 6, d), lambda i, j: (_mod_row(i, tm), 0, 0)),
                pl.BlockSpec((d, tn), lambda i, j: (0, j))]
    args = [x, mods_l, w]
    kern = _inproj_kernel
    if rope is not None:
        first_lat_tile = T_CTX // tm
        tiles_per_seq = DEC_SEQ // tm
        gpt = tn // LANES
        plan = {}
        for col0, width, kind, scale in ((QA0, MIX_HALF, 'a', 1.0), (KA0, MIX_HALF, 'a', 1.0),
                                         (QB0, B_QK_W, 'b', 1.0), (KB0, B_QK_W, 'b', B_QK_DIM ** -0.5)):
            for g in range(col0 // LANES, (col0 + width) // LANES):
                plan.setdefault(g // gpt, []).append((g % gpt, kind, scale))
        kern = functools.partial(_inproj_rope_kernel, first_lat_tile=first_lat_tile, plan=plan)
        pos = lambda i, j: (jnp.maximum(i - first_lat_tile, 0) % tiles_per_seq, 0)
        in_specs += [pl.BlockSpec((tm, LANES), pos)] * 4
        args += list(rope)
    return pl.pallas_call(
        kern,
        grid=(t // tm, n // tn),
        in_specs=in_specs,
        out_specs=pl.BlockSpec((tm, tn), lambda i, j: (i, j)),
        out_shape=jax.ShapeDtypeStruct((t, n), F32),
        scratch_shapes=[pltpu.VMEM((tm, d), BF16)],
        compiler_params=_cparams(("parallel", "arbitrary")),
        name="inproj",
    )(*args)


def _layer_norm_rows(z, g, b):
    mu = jnp.mean(z, axis=-1, keepdims=True)
    zc = z - mu
    var = jnp.mean(zc * zc, axis=-1, keepdims=True)
    return zc * lax.rsqrt(var + LN_EPS) * g + b


def _outproj_kernel(ma_ref, mb_ref, x_ref, mod_ref, wa_ref, wb_ref, g_ref, b_ref, x1_ref, h2t_ref):
    o = jnp.dot(ma_ref[...], wa_ref[...], preferred_element_type=F32)
    o = o + jnp.dot(mb_ref[...], wb_ref[...], preferred_element_type=F32)
    g1 = mod_ref[2:3, :]
    x1 = _layer_norm_rows(DEEPNORM_ALPHA * x_ref[...] + g1 * o, g_ref[...], b_ref[...])
    x1_ref[...] = x1
    h2 = x1 * (1.0 + mod_ref[4:5, :]) + mod_ref[3:4, :]
    h2t_ref[...] = h2.T.astype(BF16)


def _outproj(mix_a, mix_b, x, mods_l, w_out, ln_g, ln_b, tm=512):
    t, d = x.shape
    ka = mix_a.shape[1]
    kb = mix_b.shape[1]
    return pl.pallas_call(
        _outproj_kernel,
        grid=(t // tm,),
        in_specs=[pl.BlockSpec((tm, ka), lambda i: (i, 0)),
                  pl.BlockSpec((tm, kb), lambda i: (i, 0)),
                  pl.BlockSpec((tm, d), lambda i: (i, 0)),
                  pl.BlockSpec((None, 6, d), lambda i: (_mod_row(i, tm), 0, 0)),
                  pl.BlockSpec((ka, d), lambda i: (0, 0)),
                  pl.BlockSpec((kb, d), lambda i: (1, 0)),
                  pl.BlockSpec((1, d), lambda i: (0, 0)),
                  pl.BlockSpec((1, d), lambda i: (0, 0))],
        out_specs=[pl.BlockSpec((tm, d), lambda i: (i, 0)),
                   pl.BlockSpec((d, tm), lambda i: (0, i))],
        out_shape=[jax.ShapeDtypeStruct((t, d), F32),
                   jax.ShapeDtypeStruct((d, t), BF16)],
        compiler_params=_cparams(("parallel",)),
        name="outproj_ln",
    )(mix_a, mix_b, x, mods_l, w_out, w_out, ln_g, ln_b)


def _split_bf16(x):
    hi = x.astype(BF16)
    lo = (x - hi.astype(F32)).astype(BF16)
    return hi, lo


def _wqk_kernel(k_ref, wq_ref, o_ref):
    kh, kl = _split_bf16(k_ref[...])
    wh, wl = _split_bf16(wq_ref[...])
    nt = (((1,), (1,)), ((), ()))
    acc = lax.dot_general(kh, wh, nt, preferred_element_type=F32)
    acc = acc + lax.dot_general(kh, wl, nt, preferred_element_type=F32)
    acc = acc + lax.dot_general(kl, wh, nt, preferred_element_type=F32)
    o_ref[...] = acc.astype(BF16)


def _wqk_fold(keys_l, wq_l):
    h, p, nk, half = keys_l.shape
    d = wq_l.shape[0]
    return pl.pallas_call(
        _wqk_kernel,
        grid=(h * p,),
        in_specs=[pl.BlockSpec((None, nk, half), lambda i: (i, 0, 0)),
                  pl.BlockSpec((d, half), lambda i: (0, i))],
        out_specs=pl.BlockSpec((nk, d), lambda i: (i, 0)),
        out_shape=jax.ShapeDtypeStruct((h * p * nk, d), BF16),
        compiler_params=_cparams(("parallel",)),
        name="peer_wqk_fold",
    )(keys_l.reshape(h * p, nk, half), wq_l)


def _scores_kernel(w_ref, h_ref, o_ref):
    o_ref[...] = jnp.dot(w_ref[...], h_ref[...], preferred_element_type=F32)


def _scores(wqk_t, h2t, tn=512):
    n, d = wqk_t.shape
    t = h2t.shape[1]
    return pl.pallas_call(
        _scores_kernel,
        grid=(t // tn,),
        in_specs=[pl.BlockSpec((n, d), lambda i: (0, 0)),
                  pl.BlockSpec((d, tn), lambda i: (0, i))],
        out_specs=pl.BlockSpec((n, tn), lambda i: (0, i)),
        out_shape=jax.ShapeDtypeStruct((n, t), F32),
        compiler_params=_cparams(("parallel",)),
        name="peer_scores",
    )(wqk_t, h2t)


def _gelu_tanh(x):
    c = math.sqrt(2.0 / math.pi)
    return 0.5 * x * (1.0 + jnp.tanh(c * (x + 0.044715 * (x * x * x))))


def _peer_kernel(h2t_ref, u_ref, vt_ref, r2_ref, e2_ref, lim_ref, w1_ref, x1_ref, mod_ref, g_ref, b_ref,
                 o_ref, acc_ref, w_ref, *, n_i, sub_i, n_heads, nk):
    eb = pl.program_id(1)

    @pl.when(eb == 0)
    def _():
        acc_ref[...] = jnp.zeros_like(acc_ref)

    for sb in range(n_i // sub_i):
        rows = slice(sb * sub_i * nk, (sb + 1) * sub_i * nk)
        a = jnp.dot(u_ref[rows, :], h2t_ref[...], preferred_element_type=F32)
        for si in range(sub_i):
            ii = sb * sub_i + si
            gate = None
            for h in range(n_heads):
                lim = lim_ref[h, ii:ii + 1, :].astype(BF16)
                w1 = w1_ref[h, ii:ii + 1, :].astype(BF16)
                term = jnp.where(r2_ref[h] < lim, e2_ref[h], jnp.zeros((), BF16)) * w1
                gate = term if gate is None else gate + term
            act = _gelu_tanh(a[si * nk:(si + 1) * nk, :]).astype(BF16)
            w_ref[ii * nk:(ii + 1) * nk, :] = act * gate
    acc_ref[...] += jnp.dot(vt_ref[...], w_ref[...], preferred_element_type=F32)

    @pl.when(eb == pl.num_programs(1) - 1)
    def _():
        g2 = mod_ref[5:6, :]
        tm = acc_ref.shape[1]
        for c in range(tm // LANES):
            rows = slice(c * LANES, (c + 1) * LANES)
            y = acc_ref[:, rows].T
            z = DEEPNORM_ALPHA * x1_ref[rows, :] + g2 * y
            o_ref[rows, :] = _layer_norm_rows(z, g_ref[...], b_ref[...])


def _peer_dense(h2t, u, vt, r2t, e2t, lim1t, w1t, x1, mods_l, ln_g, ln_b, row0=0, nrows=None,
                tm=512, n_i=8, sub_i=4):
    d, t = h2t.shape
    nrows = t if nrows is None else nrows
    nexp = u.shape[0]
    n_heads, nk, _ = r2t.shape
    te = n_i * nk
    t0 = row0 // tm
    kern = functools.partial(_peer_kernel, n_i=n_i, sub_i=sub_i, n_heads=n_heads, nk=nk)
    return pl.pallas_call(
        kern,
        grid=(nrows // tm, nexp // te),
        in_specs=[pl.BlockSpec((d, tm), lambda i, e: (0, t0 + i)),
                  pl.BlockSpec((te, d), lambda i, e: (e, 0)),
                  pl.BlockSpec((d, te), lambda i, e: (0, e)),
                  pl.BlockSpec((n_heads, nk, tm), lambda i, e: (0, 0, t0 + i)),
                  pl.BlockSpec((n_heads, nk, tm), lambda i, e: (0, 0, t0 + i)),
                  pl.BlockSpec((n_heads, n_i, tm), lambda i, e: (0, e, t0 + i)),
                  pl.BlockSpec((n_heads, n_i, tm), lambda i, e: (0, e, t0 + i)),
                  pl.BlockSpec((tm, d), lambda i, e: (t0 + i, 0)),
                  pl.BlockSpec((None, 6, d), lambda i, e: (_mod_row(t0 + i, tm), 0, 0)),
                  pl.BlockSpec((1, d), lambda i, e: (0, 0)),
                  pl.BlockSpec((1, d), lambda i, e: (0, 0))],
        out_specs=pl.BlockSpec((tm, d), lambda i, e: (i, 0)),
        out_shape=jax.ShapeDtypeStruct((nrows, d), F32),
        scratch_shapes=[pltpu.VMEM((d, tm), F32), pltpu.VMEM((te, tm), BF16)],
        compiler_params=_cparams(("parallel", "arbitrary")),
        name="peer_dense",
    )(h2t, u, vt, r2t, e2t, lim1t, w1t, x1, mods_l, ln_g, ln_b)


def _topk_rows(s, k, stable):
    n, lanes = s.shape
    row = lax.broadcasted_iota(jnp.int32, s.shape, 0).astype(F32)
    row_k = lax.broadcasted_iota(jnp.int32, (k, lanes), 0)
    rank = jnp.full(s.shape, float(k), F32)
    vals = jnp.zeros((k, lanes), F32)
    for r in range(k):
        m = jnp.max(s, axis=0, keepdims=True)
        sel = s == m
        if stable:
            first = jnp.min(jnp.where(sel, row, float(n)), axis=0, keepdims=True)
            sel = row == first
        rank = jnp.where(sel, float(r), rank)
        vals = jnp.where(row_k == r, m, vals)
        s = jnp.where(sel, -jnp.inf, s)
    return rank, vals


_CAND_ROWS = 16 + 7 * 8 + 8


def _route_tile(s1, s2, stable):
    k = PEER_TOPK
    rank1, v1 = _topk_rows(s1, k, stable)
    rank2, v2 = _topk_rows(s2, k, stable)

    groups = [v1[0:1, :] + v2] + [v1[a:a + 1, :] + v2[0:8, :] for a in range(1, 8)] + [v1[8:16, :] + v2[0:1, :]]
    cand = jnp.concatenate(groups, axis=0)
    r = lax.broadcasted_iota(jnp.int32, cand.shape, 0)
    flat = jnp.where(r < 16, r, jnp.where(r < 72, (1 + (r - 16) // 8) * k + (r - 16) % 8, (r - 64) * k)).astype(F32)
    top = v1[0:1, :] + v2[0:1, :]
    picked = jnp.zeros(cand.shape, F32)
    z = jnp.zeros((1, cand.shape[1]), F32)
    for _ in range(k):
        m = jnp.max(cand, axis=0, keepdims=True)
        sel = cand == m
        if stable:
            first = jnp.min(jnp.where(sel, flat, float(k * k)), axis=0, keepdims=True)
            sel = flat == first
        picked = jnp.where(sel, 1.0, picked)
        z = z + jnp.exp(m - top)
        cand = jnp.where(sel, -jnp.inf, cand)

    n_rows = [jnp.sum(picked[0:16, :], axis=0, keepdims=True)]
    n_rows += [jnp.sum(picked[16 + 8 * (a - 1):16 + 8 * a, :], axis=0, keepdims=True) for a in range(1, 8)]
    n_rows += [picked[64 + a:65 + a, :] for a in range(8, 16)]
    lim = jnp.zeros(s1.shape, F32)
    for a in range(k):
        lim = jnp.where(rank1 == float(a), n_rows[a], lim)

    def n_selected(rank):
        return jnp.sum(jnp.where(rank < float(k), 1.0, 0.0), axis=0, keepdims=True)

    tie = ((n_selected(rank1) != float(k)) | (n_selected(rank2) != float(k))
           | (jnp.sum(picked, axis=0, keepdims=True) != float(k)))
    e2 = jnp.exp(s2 - v2[0:1, :])
    w1 = jnp.exp(s1 - v1[0:1, :]) * (1.0 / z)
    return rank2, e2, lim, w1, tie


def _routing_kernel(s_ref, r2_ref, e2_ref, lim_ref, w1_ref, *, n_heads, nk, n_sub, sw):
    def body(it, carry):
        h = it // n_sub
        lanes = pl.ds(pl.multiple_of((it % n_sub) * sw, sw), sw)
        rows1 = pl.ds(pl.multiple_of(h * 2 * nk, nk), nk)
        rows2 = pl.ds(pl.multiple_of(h * 2 * nk + nk, nk), nk)

        def route(stable):
            rank2, e2, lim, w1, tie = _route_tile(s_ref[rows1, lanes], s_ref[rows2, lanes], stable)
            r2_ref[h, :, lanes] = rank2.astype(BF16)
            e2_ref[h, :, lanes] = e2.astype(BF16)
            lim_ref[h, :, lanes] = lim
            w1_ref[h, :, lanes] = w1
            return tie

        tie = route(stable=False)

        @pl.when(jnp.max(jnp.where(tie, 1.0, 0.0)) > 0.0)
        def _():
            route(stable=True)

        return carry

    lax.fori_loop(0, n_heads * n_sub, body, 0)


def _peer_routing(scores_t, tl=512, sw=256):
    assert PEER_TOPK == 16 and _CAND_ROWS == 80
    n, t = scores_t.shape
    nh, nk = PEER_HEADS, PEER_NKEYS
    kern = functools.partial(_routing_kernel, n_heads=nh, nk=nk, n_sub=tl // sw, sw=sw)
    spec = pl.BlockSpec((nh, nk, tl), lambda i: (0, 0, i))
    return pl.pallas_call(
        kern,
        grid=(t // tl,),
        in_specs=[pl.BlockSpec((n, tl), lambda i: (0, i))],
        out_specs=[spec, spec, spec, spec],
        out_shape=[jax.ShapeDtypeStruct((nh, nk, t), BF16), jax.ShapeDtypeStruct((nh, nk, t), BF16),
                   jax.ShapeDtypeStruct((nh, nk, t), F32), jax.ShapeDtypeStruct((nh, nk, t), F32)],
        compiler_params=_cparams(("parallel",)),
        name="peer_routing",
    )(scores_t)


def _rope_tables(n_tokens, dim):
    rows = n_tokens // GRID_W
    row = jnp.repeat(jnp.arange(rows, dtype=F32), GRID_W)
    col = jnp.tile(jnp.arange(GRID_W, dtype=F32), rows)
    quarter = dim // 4
    freqs = ROPE_BASE ** (-jnp.arange(quarter, dtype=F32) / quarter)
    ang = jnp.concatenate([row[:, None] * freqs, col[:, None] * freqs], axis=-1)
    cos, sin = jnp.cos(ang), jnp.sin(ang)
    reps = LANES // dim
    cos_full = jnp.tile(jnp.concatenate([cos, cos], axis=-1), (1, reps))
    sin_full = jnp.tile(jnp.concatenate([-sin, sin], axis=-1), (1, reps))
    return cos_full, sin_full


def _attn_kernel(*refs, tq, l_self, l_ctx, lam_init):
    if l_ctx:
        lam_ref, q_ref, k_ref, v_ref, ck_ref, cv_ref, g_ref, o_ref, ks_ref, vs_ref = refs
    else:
        lam_ref, q_ref, k_ref, v_ref, g_ref, o_ref, ks_ref, vs_ref = refs

    @pl.when(pl.program_id(2) == 0)
    def _():
        if l_ctx:
            ks_ref[0:l_ctx, :] = ck_ref[...].astype(BF16)
            vs_ref[0:l_ctx, :] = cv_ref[...].astype(BF16)
        ks_ref[l_ctx:l_ctx + l_self, :] = k_ref[...].astype(BF16)
        vs_ref[l_ctx:l_ctx + l_self, :] = v_ref[...].astype(BF16)

    lv = lam_ref[...]
    lam = (jnp.exp(jnp.sum(lv[0:1, :] * lv[1:2, :], axis=-1, keepdims=True))
           - jnp.exp(jnp.sum(lv[2:3, :] * lv[3:4, :], axis=-1, keepdims=True)) + lam_init)
    q = q_ref[...].astype(F32) * (A_QK_DIM ** -0.5)
    lane = lax.broadcasted_iota(jnp.int32, q.shape, 1)
    first = lane < A_QK_DIM
    q0 = jnp.where(first, q, 0.0).astype(BF16)
    q1 = jnp.where(first, 0.0, q).astype(BF16)
    qs = jnp.concatenate([q0, q1], axis=0)
    s = lax.dot_general(qs, ks_ref[...], (((1,), (1,)), ((), ())), preferred_element_type=F32)
    m = jnp.max(s, axis=-1, keepdims=True)
    p = jnp.exp(s - m)
    p = p / jnp.sum(p, axis=-1, keepdims=True)
    w = (p[0:tq, :] - lam * p[tq:2 * tq, :]).astype(BF16)
    o = jnp.dot(w, vs_ref[...], preferred_element_type=F32)
    o = o * lax.rsqrt(jnp.mean(o * o, axis=-1, keepdims=True) + LN_EPS) * g_ref[...] * (1.0 - lam_init)
    o_ref[...] = o.astype(o_ref.dtype)


def _diff_attention(q_src, k_src, v_src, nb, l_self, lam_vec, norm_g, lam_init, ctx_k=None, ctx_v=None, tq=256):
    (q_arr, q_col0, q_row0), (k_arr, k_col0, k_row0), (v_arr, v_col0, v_row0) = q_src, k_src, v_src
    l_ctx = 0 if ctx_k is None else ctx_k.shape[2]
    nq = l_self // tq
    qb0, kb0, vb0 = q_col0 // LANES, k_col0 // LANES, v_col0 // LANES
    rq0, rk0, rv0 = q_row0 // tq, k_row0 // l_self, v_row0 // l_self
    kern = functools.partial(_attn_kernel, tq=tq, l_self=l_self, l_ctx=l_ctx, lam_init=lam_init)
    in_specs = [pl.BlockSpec((4, A_QK_DIM), lambda b, h, i: (0, 0)),
                pl.BlockSpec((tq, LANES), lambda b, h, i: (rq0 + b * nq + i, qb0 + h)),
                pl.BlockSpec((l_self, LANES), lambda b, h, i: (rk0 + b, kb0 + h)),
                pl.BlockSpec((l_self, LANES), lambda b, h, i: (rv0 + b, vb0 + h))]
    args = [lam_vec, q_arr, k_arr, v_arr]
    if l_ctx:
        in_specs += [pl.BlockSpec((None, None, l_ctx, LANES), lambda b, h, i: (b, h, 0, 0)),
                     pl.BlockSpec((None, None, l_ctx, LANES), lambda b, h, i: (b, h, 0, 0))]
        args += [ctx_k, ctx_v]
    in_specs += [pl.BlockSpec((None, 1, LANES), lambda b, h, i: (h, 0, 0))]
    args += [norm_g.reshape(A_HEADS, 1, A_V_DIM)]
    return pl.pallas_call(
        kern,
        grid=(nb, A_HEADS, nq),
        in_specs=in_specs,
        out_specs=pl.BlockSpec((tq, LANES), lambda b, h, i: (b * nq + i, h)),
        out_shape=jax.ShapeDtypeStruct((nb * l_self, A_HEADS * A_V_DIM), BF16),
        scratch_shapes=[pltpu.VMEM((l_ctx + l_self, LANES), BF16),
                        pltpu.VMEM((l_ctx + l_self, LANES), BF16)],
        compiler_params=_cparams(("parallel", "parallel", "arbitrary")),
        name="diff_attention",
    )(*args)


def _ret_kernel(*refs, l_self, has_state, k_scale, hpb):
    if has_state:
        dec_ref, q_ref, k_ref, v_ref, g_ref, ng_ref, s0_ref, o_ref, st_ref, of_ref = refs
    else:
        dec_ref, q_ref, k_ref, v_ref, g_ref, ng_ref, o_ref, st_ref, of_ref = refs
    c = RET_CHUNK
    nc = l_self // c
    pos_i = lax.broadcasted_iota(jnp.int32, (c, c), 0).astype(F32)
    pos_j = lax.broadcasted_iota(jnp.int32, (c, c), 1).astype(F32)
    col = lax.broadcasted_iota(jnp.int32, (c, 1), 0).astype(F32)
    nt = (((1,), (1,)), ((), ()))

    def direction(d):
        dist = (pos_i - pos_j) if d == 0 else (pos_j - pos_i)
        consts = []
        for hh in range(hpb):
            lg_row = jax.nn.log_sigmoid(dec_ref[hh, d:d + 1, :])
            lg1 = lg_row[:, 0:1]
            intra = jnp.where(dist >= 0, jnp.exp(lg_row * jnp.maximum(dist, 0.0)), 0.0)
            q_dec = jnp.exp(lg1 * (col + 1.0)) if d == 0 else jnp.exp(lg1 * (c - col))
            k_dec = jnp.exp(lg1 * (c - 1.0 - col)) if d == 0 else jnp.exp(lg1 * col)
            consts.append((intra, q_dec, k_dec, jnp.exp(lg1 * float(c))))
        s_init = tuple(s0_ref[d, hh].astype(F32) if has_state else jnp.zeros((B_QK_DIM, B_V_DIM), F32)
                       for hh in range(hpb))

        def step(it, states):
            ci = it if d == 0 else nc - 1 - it
            rows = pl.ds(pl.multiple_of(ci * c, c), c)
            new_states = []
            for hh in range(hpb):
                intra, q_dec, k_dec, chunk_dec = consts[hh]
                s = states[hh]
                qk = slice(hh * B_QK_DIM, (hh + 1) * B_QK_DIM)
                vv = slice(hh * B_V_DIM, (hh + 1) * B_V_DIM)
                qc = q_ref[rows, qk].astype(BF16)
                kf = k_ref[rows, qk].astype(F32) * k_scale
                vc = v_ref[rows, vv].astype(BF16)
                sc = lax.dot_general(qc, kf.astype(BF16), nt, preferred_element_type=F32) * intra
                o = jnp.dot(sc.astype(BF16), vc, preferred_element_type=F32)
                o = o + jnp.dot(qc, s.astype(BF16), preferred_element_type=F32) * q_dec
                if d == 0:
                    of_ref[rows, vv] = o
                else:
                    o = o + of_ref[rows, vv]
                    mu = jnp.mean(o, axis=-1, keepdims=True)
                    oc = o - mu
                    var = jnp.mean(oc * oc, axis=-1, keepdims=True)
                    gate = g_ref[rows, vv]
                    gate = gate * jax.nn.sigmoid(gate)
                    o_ref[rows, vv] = (oc * lax.rsqrt(var + LN_EPS) * ng_ref[hh] * gate).astype(o_ref.dtype)
                kd = (kf * k_dec).T.astype(BF16)
                new_states.append(s * chunk_dec + jnp.dot(kd, vc, preferred_element_type=F32))
            return tuple(new_states)

        final = lax.fori_loop(0, nc, step, s_init)
        for hh in range(hpb):
            st_ref[d, hh] = final[hh]

    direction(0)
    direction(1)


def _retention(q_src, k_src, v_src, g_src, nb, l_self, decay_logit, norm_g, k_scale, s0=None, hpb=1):
    (q_arr, q_col0, q_row0), (k_arr, k_col0, k_row0) = q_src, k_src
    (v_arr, v_col0, v_row0), (g_arr, g_col0, g_row0) = v_src, g_src
    has_state = s0 is not None
    kern = functools.partial(_ret_kernel, l_self=l_self, has_state=has_state, k_scale=k_scale, hpb=hpb)
    dec = jnp.broadcast_to(decay_logit.T[:, :, None], (B_HEADS, 2, LANES)).astype(F32)
    qw, vw = hpb * B_QK_DIM, hpb * B_V_DIM
    qc0, kc0 = q_col0 // qw, k_col0 // qw
    vc0, gc0 = v_col0 // vw, g_col0 // vw
    rq, rk, rv, rg = q_row0 // l_self, k_row0 // l_self, v_row0 // l_self, g_row0 // l_self
    in_specs = [pl.BlockSpec((hpb, 2, LANES), lambda b, h: (h, 0, 0)),
                pl.BlockSpec((l_self, qw), lambda b, h: (rq + b, qc0 + h)),
                pl.BlockSpec((l_self, qw), lambda b, h: (rk + b, kc0 + h)),
                pl.BlockSpec((l_self, vw), lambda b, h: (rv + b, vc0 + h)),
                pl.BlockSpec((l_self, vw), lambda b, h: (rg + b, gc0 + h)),
                pl.BlockSpec((hpb, 1, B_V_DIM), lambda b, h: (h, 0, 0))]
    args = [dec, q_arr, k_arr, v_arr, g_arr, norm_g.reshape(B_HEADS, 1, B_V_DIM)]
    state_spec = pl.BlockSpec((None, 2, hpb, B_QK_DIM, B_V_DIM), lambda b, h: (b, 0, h, 0, 0))
    if has_state:
        in_specs.append(state_spec)
        args.append(s0)
    return pl.pallas_call(
        kern,
        grid=(nb, B_HEADS // hpb),
        in_specs=in_specs,
        out_specs=[pl.BlockSpec((l_self, vw), lambda b, h: (b, h)), state_spec],
        out_shape=[jax.ShapeDtypeStruct((nb * l_self, B_HEADS * B_V_DIM), BF16),
                   jax.ShapeDtypeStruct((nb, 2, B_HEADS, B_QK_DIM, B_V_DIM), F32)],
        scratch_shapes=[pltpu.VMEM((l_self, vw), F32)],
        compiler_params=_cparams(("parallel", "parallel")),
        name="retention",
    )(*args)


def _sgu_kernel(u_ref, v_ref, g_ref, b_ref, ws_ref, bs_ref, o_ref):
    vn = _layer_norm_rows(v_ref[...], g_ref[...], b_ref[...]).astype(BF16)
    tm = vn.shape[0]
    gw = C_WIDTH // C_GROUPS
    for gi in range(C_GROUPS):
        w = ws_ref[gi].astype(BF16)
        bias = bs_ref[:, gi:gi + 1]
        cols = slice(gi * gw, (gi + 1) * gw)
        for ci in range(tm // C_CHUNK):
            rows = slice(ci * C_CHUNK, (ci + 1) * C_CHUNK)
            mixed = jnp.dot(w, vn[rows, cols], preferred_element_type=F32) + bias
            o_ref[rows, cols] = (u_ref[rows, cols] * mixed).astype(o_ref.dtype)


def _sgu(proj, sg_g, sg_b, w_s, b_s, tm=512):
    t = proj.shape[0]
    return pl.pallas_call(
        _sgu_kernel,
        grid=(t // tm,),
        in_specs=[pl.BlockSpec((tm, C_WIDTH), lambda i: (i, 0)),
                  pl.BlockSpec((tm, C_WIDTH), lambda i: (i, 1)),
                  pl.BlockSpec((1, C_WIDTH), lambda i: (0, 0)),
                  pl.BlockSpec((1, C_WIDTH), lambda i: (0, 0)),
                  pl.BlockSpec((C_GROUPS, C_CHUNK, C_CHUNK), lambda i: (0, 0, 0)),
                  pl.BlockSpec((C_CHUNK, C_GROUPS), lambda i: (0, 0))],
        out_specs=pl.BlockSpec((tm, C_WIDTH), lambda i: (i, 0)),
        out_shape=jax.ShapeDtypeStruct((t, C_WIDTH), BF16),
        compiler_params=_cparams(("parallel",)),
        name="sgu",
    )(proj, proj, sg_g[None, :], sg_b[None, :], w_s, b_s.T)


def _pool_kernel(p_ref, w_ref, sc_ref, o_ref, *, l_seq):
    p = p_ref[...]
    n = p.shape[0]
    pos = (lax.broadcasted_iota(jnp.int32, p.shape, 0) % l_seq)
    w_mat = w_ref[...].astype(BF16)
    for gi, win in enumerate(POOL_WINDOWS):
        @pl.when(pl.program_id(1) == gi)
        def _(win=win):
            half = win // 2
            acc = jnp.zeros(p.shape, F32)
            for d in range(-half, half):
                shifted = p if d == 0 else pltpu.roll(p, (-d) % n, 0)
                ok = (pos + d >= 0) & (pos + d <= l_seq - 1)
                acc = acc + jnp.where(ok, shifted, 0.0)
            cnt = jnp.minimum(pos + (half - 1), l_seq - 1) - jnp.maximum(pos - half, 0) + 1
            pooled = acc / cnt.astype(F32) - p
            mixed = jnp.dot(pooled.astype(BF16), w_mat, preferred_element_type=F32)
            o_ref[...] = (mixed * sc_ref[...]).astype(o_ref.dtype)


def _pool(proj, col0, row0, nrows, l_seq, p_w, p_scale, rows_per_step):
    kern = functools.partial(_pool_kernel, l_seq=l_seq)
    cb0 = col0 // D_GROUP
    rb0 = row0 // rows_per_step
    return pl.pallas_call(
        kern,
        grid=(nrows // rows_per_step, N_POOL),
        in_specs=[pl.BlockSpec((rows_per_step, D_GROUP), lambda i, g: (rb0 + i, cb0 + g)),
                  pl.BlockSpec((None, D_GROUP, D_GROUP), lambda i, g: (g, 0, 0)),
                  pl.BlockSpec((1, D_GROUP), lambda i, g: (0, g))],
        out_specs=pl.BlockSpec((rows_per_step, D_GROUP), lambda i, g: (i, g)),
        out_shape=jax.ShapeDtypeStruct((nrows, D_WIDTH), BF16),
        compiler_params=_cparams(("parallel", "parallel")),
        name="pool",
    )(proj, p_w, p_scale[None, :])


QA0, KA0, VA0 = 0, 1024, 2048
QB0, KB0, VB0, GB0 = 3072, 3584, 4096, 5120
B_QK_W = B_HEADS * B_QK_DIM


def _even_mixer(proj, e, l, diff_lambda, diff_norm_g, ret_decay_logit, ret_norm_g,
                cache_attn_k, cache_attn_v, state_retention):
    lam_init = 0.8 - 0.6 * math.exp(-0.3 * l)
    oa_ctx = _diff_attention((proj, QA0, 0), (proj, KA0, 0), (proj, VA0, 0), BATCH, SEQ,
                             diff_lambda, diff_norm_g, lam_init, tq=SEQ)
    oa_lat = _diff_attention((proj, QA0, T_CTX), (proj, KA0, T_CTX), (proj, VA0, T_CTX), DEC_BATCH, DEC_SEQ,
                             diff_lambda, diff_norm_g, lam_init,
                             ctx_k=cache_attn_k[:, e], ctx_v=cache_attn_v[:, e], tq=256)
    mix_a = jnp.concatenate([oa_ctx, oa_lat], axis=0)

    ob_ctx, st_ctx = _retention((proj, QB0, 0), (proj, KB0, 0), (proj, VB0, 0), (proj, GB0, 0), BATCH, SEQ,
                                ret_decay_logit, ret_norm_g, B_QK_DIM ** -0.5, hpb=B_HEADS)
    ob_lat, _ = _retention((proj, QB0, T_CTX), (proj, KB0, T_CTX), (proj, VB0, T_CTX), (proj, GB0, T_CTX),
                           DEC_BATCH, DEC_SEQ, ret_decay_logit, ret_norm_g, 1.0, s0=state_retention[:, e])
    mix_b = jnp.concatenate([ob_ctx, ob_lat], axis=0)

    pc = proj[:T_CTX]
    ka = pc[:, KA0:VA0].reshape(BATCH, SEQ, A_HEADS, 2 * A_QK_DIM).transpose(0, 2, 1, 3)
    va = pc[:, VA0:QB0].reshape(BATCH, SEQ, A_HEADS, A_V_DIM).transpose(0, 2, 1, 3)
    return mix_a, mix_b, ka, va, st_ctx


def _odd_mixer(proj, sg_g, sg_b, w_s, b_s, p_w, p_scale):
    mix_c = _sgu(proj, sg_g, sg_b, w_s, b_s)
    pool_ctx = _pool(proj, 2 * C_WIDTH, 0, T_CTX, SEQ, p_w, p_scale, rows_per_step=8 * SEQ)
    pool_lat = _pool(proj, 2 * C_WIDTH, T_CTX, T_LAT, DEC_SEQ, p_w, p_scale, rows_per_step=DEC_SEQ)
    return mix_c, jnp.concatenate([pool_ctx, pool_lat], axis=0)


def kernel(x_prompt, x_sample, cache_attn_k, cache_attn_v, state_retention, c, c_ctx, w_ada, b_ada, ln_g, ln_b,
           w_in_even, diff_lambda, diff_norm_g, ret_decay_logit, ret_norm_g, w_in_odd, sgu_ln_g, sgu_ln_b,
           sgu_w, sgu_b, pool_w, pool_scale, w_out, peer_wq, peer_keys, peer_u, peer_v):
    x = jnp.concatenate([x_prompt.reshape(T_CTX, D_MODEL), x_sample.reshape(T_LAT, D_MODEL)], axis=0)
    cond = jnp.concatenate([c_ctx[None, :], c, jnp.zeros((COND_PAD - N_COND, D_MODEL), F32)], axis=0)
    mods = _ada(cond, w_ada, b_ada)[:, :N_COND].reshape(DEPTH, N_COND, 6, D_MODEL)

    rope = _rope_tables(DEC_SEQ, A_QK_DIM) + _rope_tables(DEC_SEQ, B_QK_DIM)
    w_in_even_b = w_in_even.astype(BF16)
    w_in_odd_b = w_in_odd.astype(BF16)
    w_out_b = w_out.astype(BF16)
    peer_u_b = peer_u.astype(BF16)
    peer_vt_b = peer_v.astype(BF16).transpose(0, 2, 1)

    ks, vs, ss = [], [], []
    for l in range(DEPTH):
        mods_l = mods[l]
        if l % 2 == 0:
            e = l // 2
            proj = _inproj(x, mods_l, w_in_even_b[e], rope=rope)
            mix_a, mix_b, ka, va, st = _even_mixer(
                proj, e, l, diff_lambda[e], diff_norm_g[e], ret_decay_logit[e], ret_norm_g[e],
                cache_attn_k, cache_attn_v, state_retention)
            ks.append(ka)
            vs.append(va)
            ss.append(st)
        else:
            o = l // 2
            proj = _inproj(x, mods_l, w_in_odd_b[o])
            mix_a, mix_b = _odd_mixer(proj, sgu_ln_g[o], sgu_ln_b[o], sgu_w[o], sgu_b[o], pool_w[o], pool_scale[o])
        x1, h2t = _outproj(mix_a, mix_b, x, mods_l, w_out_b[l], ln_g[l, 0][None, :], ln_b[l, 0][None, :])
        wqk_t = _wqk_fold(peer_keys[l], peer_wq[l])
        scores_t = _scores(wqk_t, h2t)
        r2t, e2t, lim1t, w1t = _peer_routing(scores_t)
        dense = functools.partial(_peer_dense, h2t, peer_u_b[l], peer_vt_b[l], r2t, e2t, lim1t, w1t, x1, mods_l,
                                  ln_g[l, 1][None, :], ln_b[l, 1][None, :])
        if l < DEPTH - 1:
            x = dense()
    y_prompt = dense(row0=0, nrows=T_CTX).reshape(BATCH, SEQ, D_MODEL)
    y_sample = dense(row0=T_CTX, nrows=T_LAT).reshape(DEC_BATCH, DEC_SEQ, D_MODEL)
    return (y_prompt, y_sample, jnp.stack(ks, axis=1), jnp.stack(vs, axis=1), jnp.stack(ss, axis=1))
```

```python
import functools
import math

import jax
import jax.numpy as jnp
from jax import lax
from jax.experimental import pallas as pl
from jax.experimental.pallas import tpu as pltpu

F32 = jnp.float32
BF16 = jnp.bfloat16

D_MODEL = 2048
BATCH = 32
SEQ = 256
DEPTH = 4
DEC_BATCH = 2
DEC_SEQ = 4096
PAST_LEN = 256
GRID_W = 64
N_EVEN = (DEPTH + 1) // 2
MIX_HALF = D_MODEL // 2
A_HEADS = 8
A_QK_DIM = 64
A_V_DIM = MIX_HALF // A_HEADS
B_HEADS = 4
B_QK_DIM = 128
B_V_DIM = MIX_HALF // B_HEADS
RET_CHUNK = 128
C_WIDTH = MIX_HALF
C_GROUPS = 4
C_CHUNK = 128
D_WIDTH = MIX_HALF
POOL_WINDOWS = (2, 4, 8, 16)
N_POOL = 4
D_GROUP = D_WIDTH // N_POOL
PEER_HEADS = 8
PEER_NKEYS = 128
PEER_NEXP = PEER_NKEYS * PEER_NKEYS
PEER_QDIM = 256
PEER_TOPK = 16
ROPE_BASE = 10000.0
LN_EPS = 1e-5
DEEPNORM_ALPHA = (2 * DEPTH) ** 0.25
EVEN_IN = 6144
ODD_IN = 2 * C_WIDTH + D_WIDTH

T_CTX = BATCH * SEQ
T_LAT = DEC_BATCH * DEC_SEQ
T_ALL = T_CTX + T_LAT
N_COND = 1 + DEC_BATCH
COND_PAD = 8

LANES = 128
VMEM_LIMIT = 56 * 1024 * 1024


def _cparams(sem):
    return pltpu.CompilerParams(dimension_semantics=sem, vmem_limit_bytes=VMEM_LIMIT)


def _mod_row(i, tm):
    start = i * tm
    return jnp.where(start < T_CTX, 0, 1 + (start - T_CTX) // DEC_SEQ)


def _ada_kernel(cond_ref, w_ref, b_ref, o_ref):
    c = cond_ref[...]
    s = (c * jax.nn.sigmoid(c)).astype(BF16)
    o_ref[...] = jnp.dot(s, w_ref[...].astype(BF16), preferred_element_type=F32) + b_ref[...]


def _ada(cond, w_ada, b_ada, tn=1024):
    depth, d, n = w_ada.shape
    return pl.pallas_call(
        _ada_kernel,
        grid=(depth, n // tn),
        in_specs=[pl.BlockSpec((COND_PAD, d), lambda l, j: (0, 0)),
                  pl.BlockSpec((None, d, tn), lambda l, j: (l, 0, j)),
                  pl.BlockSpec((None, 1, tn), lambda l, j: (l, 0, j))],
        out_specs=pl.BlockSpec((None, COND_PAD, tn), lambda l, j: (l, 0, j)),
        out_shape=jax.ShapeDtypeStruct((depth, COND_PAD, n), F32),
        compiler_params=_cparams(("parallel", "parallel")),
        name="ada",
    )(cond, w_ada, b_ada.reshape(depth, 1, n))


def _modulate_into(h_ref, x_ref, mod_ref):
    @pl.when(pl.program_id(1) == 0)
    def _():
        sh = mod_ref[0:1, :]
        sc = mod_ref[1:2, :]
        h_ref[...] = (x_ref[...] * (1.0 + sc) + sh).astype(BF16)


def _inproj_kernel(x_ref, mod_ref, w_ref, o_ref, h_ref):
    _modulate_into(h_ref, x_ref, mod_ref)
    o_ref[...] = jnp.dot(h_ref[...], w_ref[...], preferred_element_type=F32)


def _rotate_lane_group(x, cos, sin, half):
    if 2 * half == LANES:
        partner = pltpu.roll(x, half, 1)
    else:
        lane = lax.broadcasted_iota(jnp.int32, x.shape, 1)
        partner = jnp.where((lane % (2 * half)) < half, pltpu.roll(x, LANES - half, 1), pltpu.roll(x, half, 1))
    return x * cos + partner * sin


def _inproj_rope_kernel(x_ref, mod_ref, w_ref, ca_ref, sa_ref, cb_ref, sb_ref, o_ref, h_ref, *, first_lat_tile, plan):
    _modulate_into(h_ref, x_ref, mod_ref)
    o_ref[...] = jnp.dot(h_ref[...], w_ref[...], preferred_element_type=F32)
    latent = pl.program_id(0) >= first_lat_tile
    for j, groups in plan.items():
        @pl.when(latent & (pl.program_id(1) == j))
        def _(groups=groups):
            for g, kind, scale in groups:
                cols = slice(g * LANES, (g + 1) * LANES)
                cos, sin = (ca_ref, sa_ref) if kind == 'a' else (cb_ref, sb_ref)
                half = (A_QK_DIM if kind == 'a' else B_QK_DIM) // 2
                y = _rotate_lane_group(o_ref[:, cols], cos[...], sin[...], half)
                o_ref[:, cols] = y if scale == 1.0 else y * scale


def _inproj(x, mods_l, w, rope=None, tm=1024, tn=1024):
    t, d = x.shape
    n = w.shape[1]
    in_specs = [pl.BlockSpec((tm, d), lambda i, j: (i, 0)),
                pl.BlockSpec((None, 6, d), lambda i, j: (_mod_row(i, tm), 0, 0)),
                pl.BlockSpec((d, tn), lambda i, j: (0, j))]
    args = [x, mods_l, w]
    kern = _inproj_kernel
    if rope is not None:
        first_lat_tile = T_CTX // tm
        tiles_per_seq = DEC_SEQ // tm
        gpt = tn // LANES
        plan = {}
        for col0, width, kind, scale in ((QA0, MIX_HALF, 'a', 1.0), (KA0, MIX_HALF, 'a', 1.0),
                                         (QB0, B_QK_W, 'b', 1.0), (KB0, B_QK_W, 'b', B_QK_DIM ** -0.5)):
            for g in range(col0 // LANES, (col0 + width) // LANES):
                plan.setdefault(g // gpt, []).append((g % gpt, kind, scale))
        kern = functools.partial(_inproj_rope_kernel, first_lat_tile=first_lat_tile, plan=plan)
        pos = lambda i, j: (jnp.maximum(i - first_lat_tile, 0) % tiles_per_seq, 0)
        in_specs += [pl.BlockSpec((tm, LANES), pos)] * 4
        args += list(rope)
    return pl.pallas_call(
        kern,
        grid=(t // tm, n // tn),
        in_specs=in_specs,
        out_specs=pl.BlockSpec((tm, tn), lambda i, j: (i, j)),
        out_shape=jax.ShapeDtypeStruct((t, n), F32),
        scratch_shapes=[pltpu.VMEM((tm, d), BF16)],
        compiler_params=_cparams(("parallel", "arbitrary")),
        name="inproj",
    )(*args)


def _layer_norm_rows(z, g, b):
    mu = jnp.mean(z, axis=-1, keepdims=True)
    zc = z - mu
    var = jnp.mean(zc * zc, axis=-1, keepdims=True)
    return zc * lax.rsqrt(var + LN_EPS) * g + b


def _outproj_kernel(ma_ref, mb_ref, x_ref, mod_ref, wa_ref, wb_ref, g_ref, b_ref, x1_ref, h2t_ref):
    o = jnp.dot(ma_ref[...], wa_ref[...], preferred_element_type=F32)
    o = o + jnp.dot(mb_ref[...], wb_ref[...], preferred_element_type=F32)
    g1 = mod_ref[2:3, :]
    x1 = _layer_norm_rows(DEEPNORM_ALPHA * x_ref[...] + g1 * o, g_ref[...], b_ref[...])
    x1_ref[...] = x1
    h2 = x1 * (1.0 + mod_ref[4:5, :]) + mod_ref[3:4, :]
    h2t_ref[...] = h2.T.astype(BF16)


def _outproj(mix_a, mix_b, x, mods_l, w_out, ln_g, ln_b, tm=512):
    t, d = x.shape
    ka = mix_a.shape[1]
    kb = mix_b.shape[1]
    return pl.pallas_call(
        _outproj_kernel,
        grid=(t // tm,),
        in_specs=[pl.BlockSpec((tm, ka), lambda i: (i, 0)),
                  pl.BlockSpec((tm, kb), lambda i: (i, 0)),
                  pl.BlockSpec((tm, d), lambda i: (i, 0)),
                  pl.BlockSpec((None, 6, d), lambda i: (_mod_row(i, tm), 0, 0)),
                  pl.BlockSpec((ka, d), lambda i: (0, 0)),
                  pl.BlockSpec((kb, d), lambda i: (1, 0)),
                  pl.BlockSpec((1, d), lambda i: (0, 0)),
                  pl.BlockSpec((1, d), lambda i: (0, 0))],
        out_specs=[pl.BlockSpec((tm, d), lambda i: (i, 0)),
                   pl.BlockSpec((d, tm), lambda i: (0, i))],
        out_shape=[jax.ShapeDtypeStruct((t, d), F32),
                   jax.ShapeDtypeStruct((d, t), BF16)],
        compiler_params=_cparams(("parallel",)),
        name="outproj_ln",
    )(mix_a, mix_b, x, mods_l, w_out, w_out, ln_g, ln_b)


def _split_bf16(x):
    hi = x.astype(BF16)
    lo = (x - hi.astype(F32)).astype(BF16)
    return hi, lo


def _wqk_kernel(k_ref, wq_ref, o_ref):
    kh, kl = _split_bf16(k_ref[...])
    wh, wl = _split_bf16(wq_ref[...])
    nt = (((1,), (1,)), ((), ()))
    acc = lax.dot_general(kh, wh, nt, preferred_element_type=F32)
    acc = acc + lax.dot_general(kh, wl, nt, preferred_element_type=F32)
    acc = acc + lax.dot_general(kl, wh, nt, preferred_element_type=F32)
    o_ref[...] = acc.astype(BF16)


def _wqk_fold(keys_l, wq_l):
    h, p, nk, half = keys_l.shape
    d = wq_l.shape[0]
    return pl.pallas_call(
        _wqk_kernel,
        grid=(h * p,),
        in_specs=[pl.BlockSpec((None, nk, half), lambda i: (i, 0, 0)),
                  pl.BlockSpec((d, half), lambda i: (0, i))],
        out_specs=pl.BlockSpec((nk, d), lambda i: (i, 0)),
        out_shape=jax.ShapeDtypeStruct((h * p * nk, d), BF16),
        compiler_params=_cparams(("parallel",)),
        name="peer_wqk_fold",
    )(keys_l.reshape(h * p, nk, half), wq_l)


def _scores_kernel(w_ref, h_ref, o_ref):
    o_ref[...] = jnp.dot(w_ref[...], h_ref[...], preferred_element_type=F32)


def _scores(wqk_t, h2t, tn=512):
    n, d = wqk_t.shape
    t = h2t.shape[1]
    return pl.pallas_call(
        _scores_kernel,
        grid=(t // tn,),
        in_specs=[pl.BlockSpec((n, d), lambda i: (0, 0)),
                  pl.BlockSpec((d, tn), lambda i: (0, i))],
        out_specs=pl.BlockSpec((n, tn), lambda i: (0, i)),
        out_shape=jax.ShapeDtypeStruct((n, t), F32),
        compiler_params=_cparams(("parallel",)),
        name="peer_scores",
    )(wqk_t, h2t)


def _gelu_tanh(x):
    c = math.sqrt(2.0 / math.pi)
    return 0.5 * x * (1.0 + jnp.tanh(c * (x + 0.044715 * (x * x * x))))


def _peer_kernel(h2t_ref, u_ref, vt_ref, r2_ref, e2_ref, lim_ref, w1_ref, x1_ref, mod_ref, g_ref, b_ref,
                 o_ref, acc_ref, w_ref, *, n_i, sub_i, n_heads, nk):
    eb = pl.program_id(1)

    @pl.when(eb == 0)
    def _():
        acc_ref[...] = jnp.zeros_like(acc_ref)

    for sb in range(n_i // sub_i):
        rows = slice(sb * sub_i * nk, (sb + 1) * sub_i * nk)
        a = jnp.dot(u_ref[rows, :], h2t_ref[...], preferred_element_type=F32)
        for si in range(sub_i):
            ii = sb * sub_i + si
            gate = None
            for h in range(n_heads):
                lim = lim_ref[h, ii:ii + 1, :].astype(BF16)
                w1 = w1_ref[h, ii:ii + 1, :].astype(BF16)
                term = jnp.where(r2_ref[h] < lim, e2_ref[h], jnp.zeros((), BF16)) * w1
                gate = term if gate is None else gate + term
            act = _gelu_tanh(a[si * nk:(si + 1) * nk, :]).astype(BF16)
            w_ref[ii * nk:(ii + 1) * nk, :] = act * gate
    acc_ref[...] += jnp.dot(vt_ref[...], w_ref[...], preferred_element_type=F32)

    @pl.when(eb == pl.num_programs(1) - 1)
    def _():
        g2 = mod_ref[5:6, :]
        tm = acc_ref.shape[1]
        for c in range(tm // LANES):
            rows = slice(c * LANES, (c + 1) * LANES)
            y = acc_ref[:, rows].T
            z = DEEPNORM_ALPHA * x1_ref[rows, :] + g2 * y
            o_ref[rows, :] = _layer_norm_rows(z, g_ref[...], b_ref[...])


def _peer_dense(h2t, u, vt, r2t, e2t, lim1t, w1t, x1, mods_l, ln_g, ln_b, row0=0, nrows=None,
                tm=512, n_i=8, sub_i=4):
    d, t = h2t.shape
    nrows = t if nrows is None else nrows
    nexp = u.shape[0]
    n_heads, nk, _ = r2t.shape
    te = n_i * nk
    t0 = row0 // tm
    kern = functools.partial(_peer_kernel, n_i=n_i, sub_i=sub_i, n_heads=n_heads, nk=nk)
    return pl.pallas_call(
        kern,
        grid=(nrows // tm, nexp // te),
        in_specs=[pl.BlockSpec((d, tm), lambda i, e: (0, t0 + i)),
                  pl.BlockSpec((te, d), lambda i, e: (e, 0)),
                  pl.BlockSpec((d, te), lambda i, e: (0, e)),
                  pl.BlockSpec((n_heads, nk, tm), lambda i, e: (0, 0, t0 + i)),
                  pl.BlockSpec((n_heads, nk, tm), lambda i, e: (0, 0, t0 + i)),
                  pl.BlockSpec((n_heads, n_i, tm), lambda i, e: (0, e, t0 + i)),
                  pl.BlockSpec((n_heads, n_i, tm), lambda i, e: (0, e, t0 + i)),
                  pl.BlockSpec((tm, d), lambda i, e: (t0 + i, 0)),
                  pl.BlockSpec((None, 6, d), lambda i, e: (_mod_row(t0 + i, tm), 0, 0)),
                  pl.BlockSpec((1, d), lambda i, e: (0, 0)),
                  pl.BlockSpec((1, d), lambda i, e: (0, 0))],
        out_specs=pl.BlockSpec((tm, d), lambda i, e: (i, 0)),
        out_shape=jax.ShapeDtypeStruct((nrows, d), F32),
        scratch_shapes=[pltpu.VMEM((d, tm), F32), pltpu.VMEM((te, tm), BF16)],
        compiler_params=_cparams(("parallel", "arbitrary")),
        name="peer_dense",
    )(h2t, u, vt, r2t, e2t, lim1t, w1t, x1, mods_l, ln_g, ln_b)


_TAKEN = 2.0 ** 100
_TAKEN_STEPS = 32.0


def _topk_rows(s, k, stable):
    n, lanes = s.shape
    row = lax.broadcasted_iota(jnp.int32, s.shape, 0).astype(F32)
    row_k = lax.broadcasted_iota(jnp.int32, (k, lanes), 0)
    rank = jnp.full(s.shape, float(k), F32)
    vals = jnp.zeros((k, lanes), F32)
    for r in range(k):
        m = jnp.max(s, axis=0, keepdims=True)
        sel = s == m
        vals = jnp.where(row_k == r, m, vals)
        if stable:
            first = jnp.min(jnp.where(sel, row, float(n)), axis=0, keepdims=True)
            sel = row == first
            rank = jnp.where(sel, float(r), rank)
            s = jnp.where(sel, -jnp.inf, s)
        else:
            s = jnp.where(sel, -_TAKEN * (1.0 + r / _TAKEN_STEPS), s)
    if not stable:
        rank = jnp.where(s <= -_TAKEN, (s * (-1.0 / _TAKEN) - 1.0) * _TAKEN_STEPS, float(k))
    return rank, vals


_CAND_ROWS = 16 + 7 * 8 + 8


def _route_tile(s1, s2, stable):
    k = PEER_TOPK
    rank1, v1 = _topk_rows(s1, k, stable)
    rank2, v2 = _topk_rows(s2, k, stable)

    groups = [v1[0:1, :] + v2] + [v1[a:a + 1, :] + v2[0:8, :] for a in range(1, 8)] + [v1[8:16, :] + v2[0:1, :]]
    cand = jnp.concatenate(groups, axis=0)
    r = lax.broadcasted_iota(jnp.int32, cand.shape, 0)
    flat = jnp.where(r < 16, r, jnp.where(r < 72, (1 + (r - 16) // 8) * k + (r - 16) % 8, (r - 64) * k)).astype(F32)
    top = v1[0:1, :] + v2[0:1, :]
    picked = jnp.zeros(cand.shape, F32)
    z = jnp.zeros((1, cand.shape[1]), F32)
    for _ in range(k):
        m = jnp.max(cand, axis=0, keepdims=True)
        sel = cand == m
        if stable:
            first = jnp.min(jnp.where(sel, flat, float(k * k)), axis=0, keepdims=True)
            sel = flat == first
        picked = jnp.where(sel, 1.0, picked)
        z = z + jnp.exp(m - top)
        cand = jnp.where(sel, -jnp.inf, cand)

    n_rows = [jnp.sum(picked[0:16, :], axis=0, keepdims=True)]
    n_rows += [jnp.sum(picked[16 + 8 * (a - 1):16 + 8 * a, :], axis=0, keepdims=True) for a in range(1, 8)]
    n_rows += [picked[64 + a:65 + a, :] for a in range(8, 16)]
    lim = jnp.zeros(s1.shape, F32)
    for a in range(k):
        lim = jnp.where(rank1 == float(a), n_rows[a], lim)

    def n_selected(rank):
        return jnp.sum(jnp.where(rank < float(k), 1.0, 0.0), axis=0, keepdims=True)

    tie = ((n_selected(rank1) != float(k)) | (n_selected(rank2) != float(k))
           | (jnp.sum(picked, axis=0, keepdims=True) != float(k)))
    e2 = jnp.exp(s2 - v2[0:1, :])
    w1 = jnp.exp(s1 - v1[0:1, :]) * (1.0 / z)
    return rank2, e2, lim, w1, tie


def _routing_kernel(s_ref, r2_ref, e2_ref, lim_ref, w1_ref, *, n_heads, nk, n_sub, sw):
    def body(it, carry):
        h = it // n_sub
        lanes = pl.ds(pl.multiple_of((it % n_sub) * sw, sw), sw)
        rows1 = pl.ds(pl.multiple_of(h * 2 * nk, nk), nk)
        rows2 = pl.ds(pl.multiple_of(h * 2 * nk + nk, nk), nk)

        def route(stable):
            rank2, e2, lim, w1, tie = _route_tile(s_ref[rows1, lanes], s_ref[rows2, lanes], stable)
            r2_ref[h, :, lanes] = rank2.astype(BF16)
            e2_ref[h, :, lanes] = e2.astype(BF16)
            lim_ref[h, :, lanes] = lim
            w1_ref[h, :, lanes] = w1
            return tie

        tie = route(stable=False)

        @pl.when(jnp.max(jnp.where(tie, 1.0, 0.0)) > 0.0)
        def _():
            route(stable=True)

        return carry

    lax.fori_loop(0, n_heads * n_sub, body, 0)


def _peer_routing(scores_t, tl=512, sw=256):
    assert PEER_TOPK == 16 and _CAND_ROWS == 80
    n, t = scores_t.shape
    nh, nk = PEER_HEADS, PEER_NKEYS
    kern = functools.partial(_routing_kernel, n_heads=nh, nk=nk, n_sub=tl // sw, sw=sw)
    spec = pl.BlockSpec((nh, nk, tl), lambda i: (0, 0, i))
    return pl.pallas_call(
        kern,
        grid=(t // tl,),
        in_specs=[pl.BlockSpec((n, tl), lambda i: (0, i))],
        out_specs=[spec, spec, spec, spec],
        out_shape=[jax.ShapeDtypeStruct((nh, nk, t), BF16), jax.ShapeDtypeStruct((nh, nk, t), BF16),
                   jax.ShapeDtypeStruct((nh, nk, t), F32), jax.ShapeDtypeStruct((nh, nk, t), F32)],
        compiler_params=_cparams(("parallel",)),
        name="peer_routing",
    )(scores_t)


def _rope_tables(n_tokens, dim):
    rows = n_tokens // GRID_W
    row = jnp.repeat(jnp.arange(rows, dtype=F32), GRID_W)
    col = jnp.tile(jnp.arange(GRID_W, dtype=F32), rows)
    quarter = dim // 4
    freqs = ROPE_BASE ** (-jnp.arange(quarter, dtype=F32) / quarter)
    ang = jnp.concatenate([row[:, None] * freqs, col[:, None] * freqs], axis=-1)
    cos, sin = jnp.cos(ang), jnp.sin(ang)
    reps = LANES // dim
    cos_full = jnp.tile(jnp.concatenate([cos, cos], axis=-1), (1, reps))
    sin_full = jnp.tile(jnp.concatenate([-sin, sin], axis=-1), (1, reps))
    return cos_full, sin_full


def _attn_kernel(*refs, tq, l_self, l_ctx, lam_init, emit_kv):
    refs = list(refs)
    ks_ref, vs_ref = refs[-2:]
    ko_ref, vo_ref = refs[-4:-2] if emit_kv else (None, None)
    if l_ctx:
        lam_ref, q_ref, k_ref, v_ref, ck_ref, cv_ref, g_ref, o_ref = refs[:8]
    else:
        lam_ref, q_ref, k_ref, v_ref, g_ref, o_ref = refs[:6]

    @pl.when(pl.program_id(2) == 0)
    def _():
        if l_ctx:
            ks_ref[0:l_ctx, :] = ck_ref[...].astype(BF16)
            vs_ref[0:l_ctx, :] = cv_ref[...].astype(BF16)
        ks_ref[l_ctx:l_ctx + l_self, :] = k_ref[...].astype(BF16)
        vs_ref[l_ctx:l_ctx + l_self, :] = v_ref[...].astype(BF16)
        if emit_kv:
            ko_ref[...] = k_ref[...].astype(F32)
            vo_ref[...] = v_ref[...].astype(F32)

    lv = lam_ref[...]
    lam = (jnp.exp(jnp.sum(lv[0:1, :] * lv[1:2, :], axis=-1, keepdims=True))
           - jnp.exp(jnp.sum(lv[2:3, :] * lv[3:4, :], axis=-1, keepdims=True)) + lam_init)
    q = q_ref[...].astype(F32) * (A_QK_DIM ** -0.5)
    lane = lax.broadcasted_iota(jnp.int32, q.shape, 1)
    first = lane < A_QK_DIM
    q0 = jnp.where(first, q, 0.0).astype(BF16)
    q1 = jnp.where(first, 0.0, q).astype(BF16)
    qs = jnp.concatenate([q0, q1], axis=0)
    s = lax.dot_general(qs, ks_ref[...], (((1,), (1,)), ((), ())), preferred_element_type=F32)
    m = jnp.max(s, axis=-1, keepdims=True)
    p = jnp.exp(s - m)
    p = p / jnp.sum(p, axis=-1, keepdims=True)
    w = (p[0:tq, :] - lam * p[tq:2 * tq, :]).astype(BF16)
    o = jnp.dot(w, vs_ref[...], preferred_element_type=F32)
    o = o * lax.rsqrt(jnp.mean(o * o, axis=-1, keepdims=True) + LN_EPS) * g_ref[...] * (1.0 - lam_init)
    o_ref[...] = o.astype(o_ref.dtype)


def _diff_attention(q_src, k_src, v_src, nb, l_self, lam_vec, norm_g, lam_init, ctx_k=None, ctx_v=None, tq=256,
                    emit_kv=False):
    (q_arr, q_col0, q_row0), (k_arr, k_col0, k_row0), (v_arr, v_col0, v_row0) = q_src, k_src, v_src
    l_ctx = 0 if ctx_k is None else ctx_k.shape[2]
    nq = l_self // tq
    qb0, kb0, vb0 = q_col0 // LANES, k_col0 // LANES, v_col0 // LANES
    rq0, rk0, rv0 = q_row0 // tq, k_row0 // l_self, v_row0 // l_self
    kern = functools.partial(_attn_kernel, tq=tq, l_self=l_self, l_ctx=l_ctx, lam_init=lam_init, emit_kv=emit_kv)
    in_specs = [pl.BlockSpec((4, A_QK_DIM), lambda b, h, i: (0, 0)),
                pl.BlockSpec((tq, LANES), lambda b, h, i: (rq0 + b * nq + i, qb0 + h)),
                pl.BlockSpec((l_self, LANES), lambda b, h, i: (rk0 + b, kb0 + h)),
                pl.BlockSpec((l_self, LANES), lambda b, h, i: (rv0 + b, vb0 + h))]
    args = [lam_vec, q_arr, k_arr, v_arr]
    if l_ctx:
        in_specs += [pl.BlockSpec((None, None, l_ctx, LANES), lambda b, h, i: (b, h, 0, 0)),
                     pl.BlockSpec((None, None, l_ctx, LANES), lambda b, h, i: (b, h, 0, 0))]
        args += [ctx_k, ctx_v]
    in_specs += [pl.BlockSpec((None, 1, LANES), lambda b, h, i: (h, 0, 0))]
    args += [norm_g.reshape(A_HEADS, 1, A_V_DIM)]
    out_specs = [pl.BlockSpec((tq, LANES), lambda b, h, i: (b * nq + i, h))]
    out_shape = [jax.ShapeDtypeStruct((nb * l_self, A_HEADS * A_V_DIM), BF16)]
    if emit_kv:
        out_specs += 2 * [pl.BlockSpec((None, None, l_self, LANES), lambda b, h, i: (b, h, 0, 0))]
        out_shape += 2 * [jax.ShapeDtypeStruct((nb, A_HEADS, l_self, LANES), F32)]
    outs = pl.pallas_call(
        kern,
        grid=(nb, A_HEADS, nq),
        in_specs=in_specs,
        out_specs=out_specs,
        out_shape=out_shape,
        scratch_shapes=[pltpu.VMEM((l_ctx + l_self, LANES), BF16),
                        pltpu.VMEM((l_ctx + l_self, LANES), BF16)],
        compiler_params=_cparams(("parallel", "parallel", "arbitrary")),
        name="diff_attention",
    )(*args)
    return outs if emit_kv else outs[0]


def _ret_kernel(*refs, l_self, has_state, k_scale, hpb):
    if has_state:
        dec_ref, q_ref, k_ref, v_ref, g_ref, ng_ref, s0_ref, o_ref, st_ref, of_ref = refs
    else:
        dec_ref, q_ref, k_ref, v_ref, g_ref, ng_ref, o_ref, st_ref, of_ref = refs
    c = RET_CHUNK
    nc = l_self // c
    pos_i = lax.broadcasted_iota(jnp.int32, (c, c), 0).astype(F32)
    pos_j = lax.broadcasted_iota(jnp.int32, (c, c), 1).astype(F32)
    col = lax.broadcasted_iota(jnp.int32, (c, 1), 0).astype(F32)
    nt = (((1,), (1,)), ((), ()))

    def direction(d):
        dist = (pos_i - pos_j) if d == 0 else (pos_j - pos_i)
        consts = []
        for hh in range(hpb):
            lg_row = jax.nn.log_sigmoid(dec_ref[hh, d:d + 1, :])
            lg1 = lg_row[:, 0:1]
            intra = jnp.where(dist >= 0, jnp.exp(lg_row * jnp.maximum(dist, 0.0)), 0.0)
            q_dec = jnp.exp(lg1 * (col + 1.0)) if d == 0 else jnp.exp(lg1 * (c - col))
            k_dec = jnp.exp(lg1 * (c - 1.0 - col)) if d == 0 else jnp.exp(lg1 * col)
            consts.append((intra, q_dec, k_dec, jnp.exp(lg1 * float(c))))
        s_init = tuple(s0_ref[d, hh].astype(F32) if has_state else jnp.zeros((B_QK_DIM, B_V_DIM), F32)
                       for hh in range(hpb))

        def step(it, states):
            ci = it if d == 0 else nc - 1 - it
            rows = pl.ds(pl.multiple_of(ci * c, c), c)
            new_states = []
            for hh in range(hpb):
                intra, q_dec, k_dec, chunk_dec = consts[hh]
                s = states[hh]
                qk = slice(hh * B_QK_DIM, (hh + 1) * B_QK_DIM)
                vv = slice(hh * B_V_DIM, (hh + 1) * B_V_DIM)
                qc = q_ref[rows, qk].astype(BF16)
                kf = k_ref[rows, qk].astype(F32) * k_scale
                vc = v_ref[rows, vv].astype(BF16)
                sc = lax.dot_general(qc, kf.astype(BF16), nt, preferred_element_type=F32) * intra
                o = jnp.dot(sc.astype(BF16), vc, preferred_element_type=F32)
                o = o + jnp.dot(qc, s.astype(BF16), preferred_element_type=F32) * q_dec
                if d == 0:
                    of_ref[rows, vv] = o
                else:
                    o = o + of_ref[rows, vv]
                    mu = jnp.mean(o, axis=-1, keepdims=True)
                    oc = o - mu
                    var = jnp.mean(oc * oc, axis=-1, keepdims=True)
                    gate = g_ref[rows, vv]
                    gate = gate * jax.nn.sigmoid(gate)
                    o_ref[rows, vv] = (oc * lax.rsqrt(var + LN_EPS) * ng_ref[hh] * gate).astype(o_ref.dtype)
                kd = (kf * k_dec).T.astype(BF16)
                new_states.append(s * chunk_dec + jnp.dot(kd, vc, preferred_element_type=F32))
            return tuple(new_states)

        final = lax.fori_loop(0, nc, step, s_init)
        for hh in range(hpb):
            st_ref[d, hh] = final[hh]

    direction(0)
    direction(1)


def _retention(q_src, k_src, v_src, g_src, nb, l_self, decay_logit, norm_g, k_scale, s0=None, hpb=1):
    (q_arr, q_col0, q_row0), (k_arr, k_col0, k_row0) = q_src, k_src
    (v_arr, v_col0, v_row0), (g_arr, g_col0, g_row0) = v_src, g_src
    has_state = s0 is not None
    kern = functools.partial(_ret_kernel, l_self=l_self, has_state=has_state, k_scale=k_scale, hpb=hpb)
    dec = jnp.broadcast_to(decay_logit.T[:, :, None], (B_HEADS, 2, LANES)).astype(F32)
    qw, vw = hpb * B_QK_DIM, hpb * B_V_DIM
    qc0, kc0 = q_col0 // qw, k_col0 // qw
    vc0, gc0 = v_col0 // vw, g_col0 // vw
    rq, rk, rv, rg = q_row0 // l_self, k_row0 // l_self, v_row0 // l_self, g_row0 // l_self
    in_specs = [pl.BlockSpec((hpb, 2, LANES), lambda b, h: (h, 0, 0)),
                pl.BlockSpec((l_self, qw), lambda b, h: (rq + b, qc0 + h)),
                pl.BlockSpec((l_self, qw), lambda b, h: (rk + b, kc0 + h)),
                pl.BlockSpec((l_self, vw), lambda b, h: (rv + b, vc0 + h)),
                pl.BlockSpec((l_self, vw), lambda b, h: (rg + b, gc0 + h)),
                pl.BlockSpec((hpb, 1, B_V_DIM), lambda b, h: (h, 0, 0))]
    args = [dec, q_arr, k_arr, v_arr, g_arr, norm_g.reshape(B_HEADS, 1, B_V_DIM)]
    state_spec = pl.BlockSpec((None, 2, hpb, B_QK_DIM, B_V_DIM), lambda b, h: (b, 0, h, 0, 0))
    if has_state:
        in_specs.append(state_spec)
        args.append(s0)
    return pl.pallas_call(
        kern,
        grid=(nb, B_HEADS // hpb),
        in_specs=in_specs,
        out_specs=[pl.BlockSpec((l_self, vw), lambda b, h: (b, h)), state_spec],
        out_shape=[jax.ShapeDtypeStruct((nb * l_self, B_HEADS * B_V_DIM), BF16),
                   jax.ShapeDtypeStruct((nb, 2, B_HEADS, B_QK_DIM, B_V_DIM), F32)],
        scratch_shapes=[pltpu.VMEM((l_self, vw), F32)],
        compiler_params=_cparams(("parallel", "parallel")),
        name="retention",
    )(*args)


def _sgu_kernel(u_ref, v_ref, g_ref, b_ref, ws_ref, bs_ref, o_ref):
    vn = _layer_norm_rows(v_ref[...], g_ref[...], b_ref[...]).astype(BF16)
    tm = vn.shape[0]
    gw = C_WIDTH // C_GROUPS
    for gi in range(C_GROUPS):
        w = ws_ref[gi].astype(BF16)
        bias = bs_ref[:, gi:gi + 1]
        cols = slice(gi * gw, (gi + 1) * gw)
        for ci in range(tm // C_CHUNK):
            rows = slice(ci * C_CHUNK, (ci + 1) * C_CHUNK)
            mixed = jnp.dot(w, vn[rows, cols], preferred_element_type=F32) + bias
            o_ref[rows, cols] = (u_ref[rows, cols] * mixed).astype(o_ref.dtype)


def _sgu(proj, sg_g, sg_b, w_s, b_s, tm=512):
    t = proj.shape[0]
    return pl.pallas_call(
        _sgu_kernel,
        grid=(t // tm,),
        in_specs=[pl.BlockSpec((tm, C_WIDTH), lambda i: (i, 0)),
                  pl.BlockSpec((tm, C_WIDTH), lambda i: (i, 1)),
                  pl.BlockSpec((1, C_WIDTH), lambda i: (0, 0)),
                  pl.BlockSpec((1, C_WIDTH), lambda i: (0, 0)),
                  pl.BlockSpec((C_GROUPS, C_CHUNK, C_CHUNK), lambda i: (0, 0, 0)),
                  pl.BlockSpec((C_CHUNK, C_GROUPS), lambda i: (0, 0))],
        out_specs=pl.BlockSpec((tm, C_WIDTH), lambda i: (i, 0)),
        out_shape=jax.ShapeDtypeStruct((t, C_WIDTH), BF16),
        compiler_params=_cparams(("parallel",)),
        name="sgu",
    )(proj, proj, sg_g[None, :], sg_b[None, :], w_s, b_s.T)


def _pool_kernel(p_ref, w_ref, sc_ref, o_ref, *, l_seq):
    p = p_ref[...]
    n = p.shape[0]
    pos = (lax.broadcasted_iota(jnp.int32, p.shape, 0) % l_seq)
    w_mat = w_ref[...].astype(BF16)
    for gi, win in enumerate(POOL_WINDOWS):
        @pl.when(pl.program_id(1) == gi)
        def _(win=win):
            half = win // 2
            acc = jnp.zeros(p.shape, F32)
            for d in range(-half, half):
                shifted = p if d == 0 else pltpu.roll(p, (-d) % n, 0)
                ok = (pos + d >= 0) & (pos + d <= l_seq - 1)
                acc = acc + jnp.where(ok, shifted, 0.0)
            cnt = jnp.minimum(pos + (half - 1), l_seq - 1) - jnp.maximum(pos - half, 0) + 1
            pooled = acc / cnt.astype(F32) - p
            mixed = jnp.dot(pooled.astype(BF16), w_mat, preferred_element_type=F32)
            o_ref[...] = (mixed * sc_ref[...]).astype(o_ref.dtype)


def _pool(proj, col0, row0, nrows, l_seq, p_w, p_scale, rows_per_step):
    kern = functools.partial(_pool_kernel, l_seq=l_seq)
    cb0 = col0 // D_GROUP
    rb0 = row0 // rows_per_step
    return pl.pallas_call(
        kern,
        grid=(nrows // rows_per_step, N_POOL),
        in_specs=[pl.BlockSpec((rows_per_step, D_GROUP), lambda i, g: (rb0 + i, cb0 + g)),
                  pl.BlockSpec((None, D_GROUP, D_GROUP), lambda i, g: (g, 0, 0)),
                  pl.BlockSpec((1, D_GROUP), lambda i, g: (0, g))],
        out_specs=pl.BlockSpec((rows_per_step, D_GROUP), lambda i, g: (i, g)),
        out_shape=jax.ShapeDtypeStruct((nrows, D_WIDTH), BF16),
        compiler_params=_cparams(("parallel", "parallel")),
        name="pool",
    )(proj, p_w, p_scale[None, :])


QA0, KA0, VA0 = 0, 1024, 2048
QB0, KB0, VB0, GB0 = 3072, 3584, 4096, 5120
B_QK_W = B_HEADS * B_QK_DIM


def _even_mixer(proj, e, l, diff_lambda, diff_norm_g, ret_decay_logit, ret_norm_g,
                cache_attn_k, cache_attn_v, state_retention):
    lam_init = 0.8 - 0.6 * math.exp(-0.3 * l)
    oa_ctx, ka, va = _diff_attention((proj, QA0, 0), (proj, KA0, 0), (proj, VA0, 0), BATCH, SEQ,
                                     diff_lambda, diff_norm_g, lam_init, tq=SEQ, emit_kv=True)
    oa_lat = _diff_attention((proj, QA0, T_CTX), (proj, KA0, T_CTX), (proj, VA0, T_CTX), DEC_BATCH, DEC_SEQ,
                             diff_lambda, diff_norm_g, lam_init,
                             ctx_k=cache_attn_k[:, e], ctx_v=cache_attn_v[:, e], tq=256)
    mix_a = jnp.concatenate([oa_ctx, oa_lat], axis=0)

    ob_ctx, st_ctx = _retention((proj, QB0, 0), (proj, KB0, 0), (proj, VB0, 0), (proj, GB0, 0), BATCH, SEQ,
                                ret_decay_logit, ret_norm_g, B_QK_DIM ** -0.5, hpb=B_HEADS)
    ob_lat, _ = _retention((proj, QB0, T_CTX), (proj, KB0, T_CTX), (proj, VB0, T_CTX), (proj, GB0, T_CTX),
                           DEC_BATCH, DEC_SEQ, ret_decay_logit, ret_norm_g, 1.0, s0=state_retention[:, e])
    mix_b = jnp.concatenate([ob_ctx, ob_lat], axis=0)

    return mix_a, mix_b, ka, va, st_ctx


def _odd_mixer(proj, sg_g, sg_b, w_s, b_s, p_w, p_scale):
    mix_c = _sgu(proj, sg_g, sg_b, w_s, b_s)
    pool_ctx = _pool(proj, 2 * C_WIDTH, 0, T_CTX, SEQ, p_w, p_scale, rows_per_step=8 * SEQ)
    pool_lat = _pool(proj, 2 * C_WIDTH, T_CTX, T_LAT, DEC_SEQ, p_w, p_scale, rows_per_step=DEC_SEQ)
    return mix_c, jnp.concatenate([pool_ctx, pool_lat], axis=0)


def kernel(x_prompt, x_sample, cache_attn_k, cache_attn_v, state_retention, c, c_ctx, w_ada, b_ada, ln_g, ln_b,
           w_in_even, diff_lambda, diff_norm_g, ret_decay_logit, ret_norm_g, w_in_odd, sgu_ln_g, sgu_ln_b,
           sgu_w, sgu_b, pool_w, pool_scale, w_out, peer_wq, peer_keys, peer_u, peer_v):
    x = jnp.concatenate([x_prompt.reshape(T_CTX, D_MODEL), x_sample.reshape(T_LAT, D_MODEL)], axis=0)
    cond = jnp.concatenate([c_ctx[None, :], c, jnp.zeros((COND_PAD - N_COND, D_MODEL), F32)], axis=0)
    mods = _ada(cond, w_ada, b_ada)[:, :N_COND].reshape(DEPTH, N_COND, 6, D_MODEL)

    rope = _rope_tables(DEC_SEQ, A_QK_DIM) + _rope_tables(DEC_SEQ, B_QK_DIM)
    w_in_even_b = w_in_even.astype(BF16)
    w_in_odd_b = w_in_odd.astype(BF16)
    w_out_b = w_out.astype(BF16)
    peer_u_b = peer_u.astype(BF16)
    peer_vt_b = peer_v.astype(BF16).transpose(0, 2, 1)

    ks, vs, ss = [], [], []
    for l in range(DEPTH):
        mods_l = mods[l]
        if l % 2 == 0:
            e = l // 2
            proj = _inproj(x, mods_l, w_in_even_b[e], rope=rope)
            mix_a, mix_b, ka, va, st = _even_mixer(
                proj, e, l, diff_lambda[e], diff_norm_g[e], ret_decay_logit[e], ret_norm_g[e],
                cache_attn_k, cache_attn_v, state_retention)
            ks.append(ka)
            vs.append(va)
            ss.append(st)
        else:
            o = l // 2
            proj = _inproj(x, mods_l, w_in_odd_b[o])
            mix_a, mix_b = _odd_mixer(proj, sgu_ln_g[o], sgu_ln_b[o], sgu_w[o], sgu_b[o], pool_w[o], pool_scale[o])
        x1, h2t = _outproj(mix_a, mix_b, x, mods_l, w_out_b[l], ln_g[l, 0][None, :], ln_b[l, 0][None, :])
        wqk_t = _wqk_fold(peer_keys[l], peer_wq[l])
        scores_t = _scores(wqk_t, h2t)
        r2t, e2t, lim1t, w1t = _peer_routing(scores_t)
        dense = functools.partial(_peer_dense, h2t, peer_u_b[l], peer_vt_b[l], r2t, e2t, lim1t, w1t, x1, mods_l,
                                  ln_g[l, 1][None, :], ln_b[l, 1][None, :])
        if l < DEPTH - 1:
            x = dense()
    y_prompt = dense(row0=0, nrows=T_CTX).reshape(BATCH, SEQ, D_MODEL)
    y_sample = dense(row0=T_CTX, nrows=T_LAT).reshape(DEC_BATCH, DEC_SEQ, D_MODEL)
    return (y_prompt, y_sample, jnp.stack(ks, axis=1), jnp.stack(vs, axis=1), jnp.stack(ss, axis=1))
```
